```python
import functools
import math
import jax
import jax.numpy as jnp
from jax import lax
import numpy as np

D_MODEL = 1024
BATCH = 4
SEQ = 4096
DEPTH = 4
DEC_BATCH = 32
DEC_SEQ = 1
PAST_LEN = 8192
PAGE_SIZE = 128

MIX_WIDTH = D_MODEL
DN_HEADS = 4
DN_HEAD_DIM = 64
DN_WIDTH = DN_HEADS * DN_HEAD_DIM
DN_CONV = 4
DN_CHUNK = 64
SB_HEADS = 8
SB_HEAD_DIM = 64
SB_WIDTH = SB_HEADS * SB_HEAD_DIM
SB_BLOCK = 128
SB_BIAS_INIT = -6.0
SSM_WIDTH = MIX_WIDTH - DN_WIDTH - SB_WIDTH
SSM_GROUP = 16
SSM_GROUPS = SSM_WIDTH // SSM_GROUP
SSM_STATE = 64
IN_WIDTH = 4 * DN_WIDTH + 2 * DN_HEADS + 3 * SB_WIDTH + SSM_WIDTH
FFN_HIDDEN = ((8 * D_MODEL + 3 * 256 - 1) // (3 * 256)) * 256
EPS = 1e-6

kernel_name = 'hybrid_deltanet_stickbreak_s5_decoder_step'


def rms_norm(x, gain):
    xf = x.astype(jnp.float32)
    y = xf * lax.rsqrt(jnp.mean(xf * xf, axis=-1, keepdims=True) + EPS)
    return (y * gain.astype(jnp.float32)).astype(x.dtype)


def l2_norm(x):
    xf = x.astype(jnp.float32)
    return xf * lax.rsqrt(jnp.sum(xf * xf, axis=-1, keepdims=True) + EPS)


def causal_conv(x, buf, w):
    xp = jnp.concatenate([buf.astype(x.dtype), x], axis=1)
    y = lax.conv_general_dilated(xp, w.astype(x.dtype)[:, None, :], window_strides=(1,), padding='VALID',
                                 dimension_numbers=('NWC', 'WIO', 'NWC'), feature_group_count=x.shape[-1])
    return y, xp[:, -(DN_CONV - 1):]


def gated_delta(q, k, v, g, beta, s0):
    bsz, t, nh, _ = k.shape
    c = min(DN_CHUNK, t)
    n = -(-t // c)
    pad = n * c - t

    def to_chunks(x):
        x = jnp.pad(x, [(0, 0), (0, pad)] + [(0, 0)] * (x.ndim - 2))
        x = x.reshape((bsz, n, c) + x.shape[2:])
        return jnp.moveaxis(jnp.moveaxis(x, 2, 3), 1, 0)

    qc, kc, vc, gc, bc = (to_chunks(a) for a in (q, k, v, g, beta))
    gam = jnp.cumsum(gc, axis=-1)
    lower = jnp.tril(jnp.ones((c, c), bool))
    strict = jnp.tril(jnp.ones((c, c), bool), -1)
    diff = gam[..., :, None] - gam[..., None, :]
    decay = jnp.where(lower, jnp.exp(jnp.where(lower, diff, 0.0)), 0.0)
    kk = jnp.einsum('nbhid,nbhjd->nbhij', kc, kc)
    tmat = jnp.eye(c, dtype=jnp.float32) + jnp.where(strict, bc[..., :, None] * kk * decay, 0.0)
    solve = functools.partial(lax.linalg.triangular_solve, left_side=True, lower=True, unit_diagonal=True)
    u = solve(tmat, bc[..., None] * vc)
    w = solve(tmat, (bc * jnp.exp(gam))[..., None] * kc)
    qk = jnp.where(lower, jnp.einsum('nbhid,nbhjd->nbhij', qc, kc) * decay, 0.0)
    q_dec = qc * jnp.exp(gam)[..., None]
    k_dec = kc * jnp.exp(gam[..., -1:] - gam)[..., None]
    g_last = jnp.exp(gam[..., -1])

    def step(s, xs):
        u_i, w_i, qk_i, qd_i, kd_i, gl_i = xs
        v_new = u_i - jnp.einsum('bhck,bhkv->bhcv', w_i, s)
        o = jnp.einsum('bhck,bhkv->bhcv', qd_i, s) + jnp.einsum('bhij,bhjv->bhiv', qk_i, v_new)
        s = gl_i[..., None, None] * s + jnp.einsum('bhck,bhcv->bhkv', kd_i, v_new)
        return s, o

    s_fin, o = lax.scan(step, s0, (u, w, qk, q_dec, k_dec, g_last))
    o = jnp.moveaxis(jnp.moveaxis(o, 0, 1), 2, 3).reshape(bsz, n * c, nh, -1)[:, :t]
    return o, s_fin


def stick_breaking(q, k, v, bias, q_pos, k_pos):
    bsz, tq, nh, dh = q.shape
    qb = min(SB_BLOCK, tq)
    nb = -(-tq // qb)
    pad = nb * qb - tq
    qp = jnp.pad(q, ((0, 0), (0, pad), (0, 0), (0, 0))).reshape(bsz, nb, qb, nh, dh)
    qp = jnp.moveaxis(qp, 1, 0).astype(jnp.float32)
    posp = jnp.pad(q_pos, (0, pad)).reshape(nb, qb)
    kf = k.astype(jnp.float32)
    vf = v.astype(jnp.float32)
    bf = bias.astype(jnp.float32)[None, :, None, None]
    scale = dh ** -0.5

    def block(args):
        q_blk, p_blk = args
        z = jnp.einsum('bqhd,bkhd->bhqk', q_blk, kf) * scale + bf
        mask = k_pos[None, :] < p_blk[:, None]
        log_beta = jax.nn.log_sigmoid(z)
        log_fail = jnp.where(mask, log_beta - z, 0.0)
        after = lax.cumsum(log_fail, axis=3, reverse=True) - log_fail
        wts = jnp.where(mask, jnp.exp(log_beta + after), 0.0)
        return jnp.einsum('bhqk,bkhd->bqhd', wts, vf)

    o = lax.map(block, (qp, posp))
    return jnp.moveaxis(o, 0, 1).reshape(bsz, nb * qb, nh, dh)[:, :tq].astype(v.dtype)


def s5(u, h0_re, h0_im, lam_re, lam_im, log_step, b_re, b_im, c_re, c_im, d_skip, glu_w, glu_b):
    f32 = jnp.float32
    bsz, t, _ = u.shape
    uf = u.astype(f32)
    ug = uf.reshape(bsz, t, SSM_GROUPS, SSM_GROUP)
    lam_re = lam_re.astype(f32)
    lam_im = lam_im.astype(f32)
    step = jnp.exp(log_step.astype(f32))[:, None]
    mag = jnp.exp(lam_re * step)
    ang = lam_im * step
    lb_re, lb_im = mag * jnp.cos(ang), mag * jnp.sin(ang)
    den = lam_re * lam_re + lam_im * lam_im
    f_re = ((lb_re - 1.0) * lam_re + lb_im * lam_im) / den
    f_im = (lb_im * lam_re - (lb_re - 1.0) * lam_im) / den
    b_re = b_re.astype(f32)
    b_im = b_im.astype(f32)
    bb_re = f_re[..., None] * b_re - f_im[..., None] * b_im
    bb_im = f_re[..., None] * b_im + f_im[..., None] * b_re
    bu_re = jnp.einsum('gpc,btgc->btgp', bb_re, ug)
    bu_im = jnp.einsum('gpc,btgc->btgp', bb_im, ug)
    h0_re = h0_re.astype(f32)
    h0_im = h0_im.astype(f32)
    bu_re = bu_re.at[:, 0].add(lb_re * h0_re - lb_im * h0_im)
    bu_im = bu_im.at[:, 0].add(lb_re * h0_im + lb_im * h0_re)
    a_re = jnp.broadcast_to(lb_re, bu_re.shape)
    a_im = jnp.broadcast_to(lb_im, bu_im.shape)

    def combine(e1, e2):
        a1r, a1i, b1r, b1i = e1
        a2r, a2i, b2r, b2i = e2
        return (a2r * a1r - a2i * a1i, a2r * a1i + a2i * a1r,
                a2r * b1r - a2i * b1i + b2r, a2r * b1i + a2i * b1r + b2i)

    _, _, x_re, x_im = lax.associative_scan(combine, (a_re, a_im, bu_re, bu_im), axis=1)
    y = (jnp.einsum('gcp,btgp->btgc', c_re.astype(f32), x_re)
         - jnp.einsum('gcp,btgp->btgc', c_im.astype(f32), x_im)).reshape(bsz, t, SSM_WIDTH)
    y = y + d_skip.astype(f32) * uf
    z = jax.nn.gelu(y)
    out = z * jax.nn.sigmoid(z @ glu_w.astype(f32) + glu_b.astype(f32))
    return out.astype(u.dtype), x_re[:, -1], x_im[:, -1]


def token_mixers(h, conv_buf, s0, h0_re, h0_im, k_past, v_past, lw):
    (w_in, conv_w, a_log, dt_bias, dn_norm, q_norm, k_norm, sb_bias, lam_re, lam_im, log_step,
     b_re, b_im, c_re, c_im, d_skip, glu_w, glu_b, w_out) = lw
    bsz, t, _ = h.shape
    f32 = jnp.float32
    p = h @ w_in
    sizes = [3 * DN_WIDTH, DN_WIDTH, DN_HEADS, DN_HEADS, 3 * SB_WIDTH]
    cuts = [int(c) for c in np.cumsum(sizes)]
    qkv_a, gate_a, alpha_a, beta_a, qkv_b, u_c = jnp.split(p, cuts, axis=-1)

    conv_out, new_conv = causal_conv(qkv_a, conv_buf, conv_w)
    r = jax.nn.silu(conv_out).reshape(bsz, t, 3, DN_HEADS, DN_HEAD_DIM)
    qa = l2_norm(r[:, :, 0]) * (DN_HEAD_DIM ** -0.5)
    ka = l2_norm(r[:, :, 1])
    va = r[:, :, 2].astype(f32)
    beta = jax.nn.sigmoid(beta_a.astype(f32))
    g = -jnp.exp(a_log.astype(f32)) * jax.nn.softplus(alpha_a.astype(f32) + dt_bias.astype(f32))
    o_a, s_new = gated_delta(qa, ka, va, g, beta, s0.astype(f32))
    o_a = rms_norm(o_a, dn_norm) * jax.nn.silu(gate_a.astype(f32).reshape(bsz, t, DN_HEADS, DN_HEAD_DIM))
    o_a = o_a.reshape(bsz, t, DN_WIDTH).astype(h.dtype)

    r = qkv_b.reshape(bsz, t, 3, SB_HEADS, SB_HEAD_DIM)
    qb = rms_norm(r[:, :, 0], q_norm)
    kb = rms_norm(r[:, :, 1], k_norm)
    vb = r[:, :, 2]
    if k_past is None:
        past = 0
        keys, vals = kb, vb
    else:
        past = k_past.shape[1]
        keys = jnp.concatenate([k_past.astype(kb.dtype), kb], axis=1)
        vals = jnp.concatenate([v_past.astype(vb.dtype), vb], axis=1)
    o_b = stick_breaking(qb, keys, vals, sb_bias, past + jnp.arange(t), jnp.arange(past + t)).reshape(bsz, t, SB_WIDTH)

    o_c, h_re, h_im = s5(u_c, h0_re, h0_im, lam_re, lam_im, log_step, b_re, b_im, c_re, c_im, d_skip, glu_w, glu_b)

    out = jnp.concatenate([o_a, o_b.astype(h.dtype), o_c], axis=-1) @ w_out
    return out, kb, vb, new_conv, s_new, h_re, h_im


def swiglu(h, w_gate, w_up, w_down):
    return (jax.nn.silu(h @ w_gate) * (h @ w_up)) @ w_down


def setup_inputs(seed: int = 0) -> dict:
    key = jax.random.key(seed)
    ks = jax.random.split(key, 40)
    f32 = jnp.float32

    def nrm(k, shape, scale=1.0):
        return jax.random.normal(k, shape, f32) * scale

    n_pages = PAST_LEN // PAGE_SIZE
    n_used = DEC_BATCH * n_pages
    n_pool = n_used + n_used // 4
    page_table = jax.random.permutation(ks[0], n_pool)[:n_used].reshape(DEC_BATCH, n_pages).astype(jnp.int32)
    dt = jnp.exp(jax.random.uniform(ks[10], (DEPTH, DN_HEADS), f32, minval=math.log(1e-3), maxval=math.log(1e-1)))
    lam_im = jnp.pi * jnp.arange(SSM_STATE, dtype=f32)
    ssm_shape = (DEPTH, SSM_GROUPS, SSM_STATE)
    return {
        'x_prompt': nrm(ks[1], (BATCH, SEQ, D_MODEL)),
        'x_sample': nrm(ks[2], (DEC_BATCH, DEC_SEQ, D_MODEL)),
        'cache_k': nrm(ks[3], (DEPTH, n_pool, PAGE_SIZE, SB_HEADS, SB_HEAD_DIM)),
        'cache_v': nrm(ks[4], (DEPTH, n_pool, PAGE_SIZE, SB_HEADS, SB_HEAD_DIM)),
        'page_table': page_table,
        'state_conv': nrm(ks[5], (DEPTH, DEC_BATCH, DN_CONV - 1, 3 * DN_WIDTH)),
        'state_delta': nrm(ks[6], (DEPTH, DEC_BATCH, DN_HEADS, DN_HEAD_DIM, DN_HEAD_DIM), 0.1),
        'state_ssm_re': nrm(ks[7], (DEPTH, DEC_BATCH, SSM_GROUPS, SSM_STATE), 0.1),
        'state_ssm_im': nrm(ks[8], (DEPTH, DEC_BATCH, SSM_GROUPS, SSM_STATE), 0.1),
        'attn_norm': 1.0 + nrm(ks[9], (DEPTH, D_MODEL), 0.02),
        'w_in': nrm(ks[11], (DEPTH, D_MODEL, IN_WIDTH), D_MODEL ** -0.5),
        'conv_w': nrm(ks[12], (DEPTH, DN_CONV, 3 * DN_WIDTH), DN_CONV ** -0.5),
        'dn_a_log': jnp.log(jax.random.uniform(ks[13], (DEPTH, DN_HEADS), f32, minval=1.0, maxval=16.0)),
        'dn_dt_bias': dt + jnp.log(-jnp.expm1(-dt)),
        'dn_out_norm': 1.0 + nrm(ks[14], (DEPTH, DN_HEAD_DIM), 0.02),
        'sb_q_norm': 1.0 + nrm(ks[15], (DEPTH, SB_HEAD_DIM), 0.02),
        'sb_k_norm': 1.0 + nrm(ks[16], (DEPTH, SB_HEAD_DIM), 0.02),
        'sb_bias': SB_BIAS_INIT + nrm(ks[32], (DEPTH, SB_HEADS), 0.1),
        'ssm_lambda_re': -0.5 + nrm(ks[17], ssm_shape, 0.01),
        'ssm_lambda_im': lam_im + nrm(ks[18], ssm_shape, 0.01),
        'ssm_log_step': jax.random.uniform(ks[19], (DEPTH, SSM_GROUPS), f32, minval=math.log(1e-3), maxval=math.log(1e-1)),
        'ssm_b_re': nrm(ks[20], (DEPTH, SSM_GROUPS, SSM_STATE, SSM_GROUP), (2 * SSM_GROUP) ** -0.5),
        'ssm_b_im': nrm(ks[21], (DEPTH, SSM_GROUPS, SSM_STATE, SSM_GROUP), (2 * SSM_GROUP) ** -0.5),
        'ssm_c_re': nrm(ks[22], (DEPTH, SSM_GROUPS, SSM_GROUP, SSM_STATE), SSM_STATE ** -0.5),
        'ssm_c_im': nrm(ks[23], (DEPTH, SSM_GROUPS, SSM_GROUP, SSM_STATE), SSM_STATE ** -0.5),
        'ssm_d': nrm(ks[24], (DEPTH, SSM_WIDTH)),
        'ssm_glu_w': nrm(ks[25], (DEPTH, SSM_WIDTH, SSM_WIDTH), SSM_WIDTH ** -0.5),
        'ssm_glu_b': nrm(ks[26], (DEPTH, SSM_WIDTH), 0.01),
        'w_out': nrm(ks[27], (DEPTH, MIX_WIDTH, D_MODEL), MIX_WIDTH ** -0.5),
        'ffn_norm': 1.0 + nrm(ks[28], (DEPTH, D_MODEL), 0.02),
        'w_gate': nrm(ks[29], (DEPTH, D_MODEL, FFN_HIDDEN), D_MODEL ** -0.5),
        'w_up': nrm(ks[30], (DEPTH, D_MODEL, FFN_HIDDEN), D_MODEL ** -0.5),
        'w_down': nrm(ks[31], (DEPTH, FFN_HIDDEN, D_MODEL), FFN_HIDDEN ** -0.5),
    }


def reference(x_prompt, x_sample, cache_k, cache_v, page_table, state_conv, state_delta, state_ssm_re, state_ssm_im,
              attn_norm, w_in, conv_w, dn_a_log, dn_dt_bias, dn_out_norm, sb_q_norm, sb_k_norm, sb_bias,
              ssm_lambda_re, ssm_lambda_im, ssm_log_step, ssm_b_re, ssm_b_im, ssm_c_re, ssm_c_im, ssm_d,
              ssm_glu_w, ssm_glu_b, w_out, ffn_norm, w_gate, w_up, w_down):
    n_prompt = x_prompt.shape[0]
    n_sample = x_sample.shape[0]
    xp, xs = x_prompt, x_sample
    acc = [[] for _ in range(12)]
    for l in range(DEPTH):
        lw = (w_in[l], conv_w[l], dn_a_log[l], dn_dt_bias[l], dn_out_norm[l], sb_q_norm[l], sb_k_norm[l], sb_bias[l],
              ssm_lambda_re[l], ssm_lambda_im[l], ssm_log_step[l], ssm_b_re[l], ssm_b_im[l],
              ssm_c_re[l], ssm_c_im[l], ssm_d[l], ssm_glu_w[l], ssm_glu_b[l], w_out[l])
        conv0 = jnp.zeros((n_prompt, DN_CONV - 1, 3 * DN_WIDTH), xp.dtype)
        s0 = jnp.zeros((n_prompt, DN_HEADS, DN_HEAD_DIM, DN_HEAD_DIM), jnp.float32)
        h0 = jnp.zeros((n_prompt, SSM_GROUPS, SSM_STATE), jnp.float32)
        mix_p, *st_p = token_mixers(rms_norm(xp, attn_norm[l]), conv0, s0, h0, h0, None, None, lw)
        xp = xp + mix_p
        xp = xp + swiglu(rms_norm(xp, ffn_norm[l]), w_gate[l], w_up[l], w_down[l])
        k_past = cache_k[l][page_table].reshape(n_sample, -1, SB_HEADS, SB_HEAD_DIM)
        v_past = cache_v[l][page_table].reshape(n_sample, -1, SB_HEADS, SB_HEAD_DIM)
        mix_s, *st_s = token_mixers(rms_norm(xs, attn_norm[l]), state_conv[l], state_delta[l],
                                    state_ssm_re[l], state_ssm_im[l], k_past, v_past, lw)
        xs = xs + mix_s
        xs = xs + swiglu(rms_norm(xs, ffn_norm[l]), w_gate[l], w_up[l], w_down[l])
        for a, val in zip(acc, st_p + st_s):
            a.append(val)
    st = [jnp.stack(a, axis=0) for a in acc]
    return (xp, xs, st[0], st[1], st[2], st[3], st[4], st[5], st[6], st[7], st[8], st[9], st[10], st[11])
```

```python
import functools
import math

import jax
import jax.numpy as jnp
from jax import lax
from jax.experimental import pallas as pl
from jax.experimental.pallas import tpu as pltpu

F32 = jnp.float32
BF16 = jnp.bfloat16

D_MODEL = 1024
DEPTH = 4
PAGE_SIZE = 128
DN_HEADS = 4
DN_HEAD_DIM = 64
DN_WIDTH = DN_HEADS * DN_HEAD_DIM
DN_CONV = 4
DN_CHUNK = 64
SB_HEADS = 8
SB_HEAD_DIM = 64
SB_WIDTH = SB_HEADS * SB_HEAD_DIM
SSM_WIDTH = D_MODEL - DN_WIDTH - SB_WIDTH
SSM_GROUP = 16
SSM_GROUPS = SSM_WIDTH // SSM_GROUP
SSM_STATE = 64
SSM_FLAT = SSM_GROUPS * SSM_STATE
FFN_HIDDEN = 2816
EPS = 1e-6

LANES = 128
SUBLANES = 8
VMEM_LIMIT = 56 * 1024 * 1024

A_COLS = 4 * DN_WIDTH
B_COLS = 3 * SB_WIDTH
IN_PAD = A_COLS + B_COLS + SSM_WIDTH + LANES


def _cparams(sem):
    return pltpu.CompilerParams(dimension_semantics=sem, vmem_limit_bytes=VMEM_LIMIT)


def _split(x, n):
    terms = []
    r = x
    for i in range(n):
        t = r.astype(BF16)
        terms.append(t)
        if i + 1 < n:
            r = r - t.astype(F32)
    return terms


def _dot(a, b):
    return jnp.dot(a, b, preferred_element_type=F32)


def _dot_l01(a01, x, n):
    return sum(_dot(a01, t) for t in _split(x, n))


def _dot_r01(x, b01, n):
    return sum(_dot(t, b01) for t in _split(x, n))


def _sigmoid(x):
    return 1.0 / (1.0 + jnp.exp(-x))


def _silu(x):
    return x * _sigmoid(x)


def _softplus(x):
    return jnp.maximum(x, 0.0) + jnp.log(1.0 + jnp.exp(-jnp.abs(x)))


def _proj_in_kernel(x_ref, g_ref, w_ref, qkn_ref, seg_ref, segt_ref,
                    a_ref, ab_ref, q16_ref, k16_ref, v16_ref, kf_ref, vf_ref, u_ref):
    x = x_ref[...]
    ms = jnp.mean(x * x, axis=-1, keepdims=True)
    h = (x * lax.rsqrt(ms + EPS) * g_ref[...]).astype(BF16)
    a_ref[...] = _dot(h, w_ref[:, 0:A_COLS])
    c0 = A_COLS
    qk = _dot(h, w_ref[:, c0:c0 + 2 * SB_WIDTH])
    v = _dot(h, w_ref[:, c0 + 2 * SB_WIDTH:c0 + B_COLS])
    c0 += B_COLS
    u_ref[...] = _dot(h, w_ref[:, c0:c0 + SSM_WIDTH])
    c0 += SSM_WIDTH
    ab_ref[...] = _dot(h, w_ref[:, c0:c0 + LANES])
    ssq = _dot_r01(qk * qk, seg_ref[...], 2)
    inv = lax.rsqrt(ssq * (1.0 / SB_HEAD_DIM) + EPS)
    qkn = qk * _dot_r01(inv, segt_ref[...], 3) * qkn_ref[...]
    qn = qkn[:, 0:SB_WIDTH]
    kn = qkn[:, SB_WIDTH:]
    q16_ref[...] = (qn * (SB_HEAD_DIM ** -0.5)).astype(BF16)
    k16_ref[...] = kn.astype(BF16)
    v16_ref[...] = v.astype(BF16)
    kf_ref[...] = kn
    vf_ref[...] = v


def _proj_in(x, gain, w16, qkn, seg, segt, tm):
    n = x.shape[0]
    row = lambda w: pl.BlockSpec((tm, w), lambda i: (i, 0))
    full = lambda a: pl.BlockSpec(a.shape, lambda i: (0,) * a.ndim)
    out_shapes = (
        jax.ShapeDtypeStruct((n, A_COLS), F32),
        jax.ShapeDtypeStruct((n, LANES), F32),
        jax.ShapeDtypeStruct((n, SB_WIDTH), BF16),
        jax.ShapeDtypeStruct((n, SB_WIDTH), BF16),
        jax.ShapeDtypeStruct((n, SB_WIDTH), BF16),
        jax.ShapeDtypeStruct((n, SB_WIDTH), F32),
        jax.ShapeDtypeStruct((n, SB_WIDTH), F32),
        jax.ShapeDtypeStruct((n, SSM_WIDTH), F32),
    )
    return pl.pallas_call(
        _proj_in_kernel,
        grid=(n // tm,),
        in_specs=[row(D_MODEL), full(gain), full(w16), full(qkn), full(seg), full(segt)],
        out_specs=(row(A_COLS), row(LANES), row(SB_WIDTH), row(SB_WIDTH), row(SB_WIDTH),
                   row(SB_WIDTH), row(SB_WIDTH), row(SSM_WIDTH)),
        out_shape=out_shapes,
        compiler_params=_cparams(("parallel",)),
        name="proj_in",
    )(x, gain, w16, qkn, seg, segt)


def _out_ffn_kernel(x_ref, oa_ref, ob_ref, oc_ref, wo_ref, g_ref, wg_ref, wu_ref, wd_ref,
                    y_ref, h_scr, acc_scr):
    j = pl.program_id(1)

    @pl.when(j == 0)
    def _():
        mix = (_dot(oa_ref[...], wo_ref[0:DN_WIDTH, :])
               + _dot(ob_ref[...], wo_ref[DN_WIDTH:DN_WIDTH + SB_WIDTH, :])
               + _dot(oc_ref[...], wo_ref[DN_WIDTH + SB_WIDTH:, :]))
        x1 = x_ref[...] + mix
        ms = jnp.mean(x1 * x1, axis=-1, keepdims=True)
        h_scr[...] = (x1 * lax.rsqrt(ms + EPS) * g_ref[...]).astype(BF16)
        acc_scr[...] = x1

    h = h_scr[...]
    gate = _dot(h, wg_ref[...])
    up = _dot(h, wu_ref[...])
    acc_scr[...] += _dot((_silu(gate) * up).astype(BF16), wd_ref[...])

    @pl.when(j == pl.num_programs(1) - 1)
    def _():
        y_ref[...] = acc_scr[...]


def _out_ffn(x, oa, ob, oc, wo16, gain, wg16, wu16, wd16, tm, th):
    n = x.shape[0]
    row = lambda w: pl.BlockSpec((tm, w), lambda i, j: (i, 0))
    return pl.pallas_call(
        _out_ffn_kernel,
        grid=(n // tm, FFN_HIDDEN // th),
        in_specs=[row(D_MODEL), row(DN_WIDTH), row(SB_WIDTH), row(SSM_WIDTH),
                  pl.BlockSpec((D_MODEL, D_MODEL), lambda i, j: (0, 0)),
                  pl.BlockSpec((1, D_MODEL), lambda i, j: (0, 0)),
                  pl.BlockSpec((D_MODEL, th), lambda i, j: (0, j)),
                  pl.BlockSpec((D_MODEL, th), lambda i, j: (0, j)),
                  pl.BlockSpec((th, D_MODEL), lambda i, j: (j, 0))],
        out_specs=row(D_MODEL),
        out_shape=jax.ShapeDtypeStruct((n, D_MODEL), F32),
        scratch_shapes=[pltpu.VMEM((tm, D_MODEL), BF16), pltpu.VMEM((tm, D_MODEL), F32)],
        compiler_params=_cparams(("parallel", "arbitrary")),
        name="out_ffn",
    )(x, oa, ob, oc, wo16, gain, wg16, wu16, wd16)


def _gelu_tanh(x):
    c = math.sqrt(2.0 / math.pi)
    return 0.5 * x * (1.0 + jnp.tanh(c * (x + 0.044715 * (x * x * x))))


def _s5_kernel(u_ref, wb_ref, lam_ref, cm_ref, d_ref, gw_ref, gb_ref, x0_ref,
               o_ref, xt_ref, bu_scr, x_scr, *, steps, nb):
    @pl.when(pl.program_id(0) == 0)
    def _():
        x_scr[...] = x0_ref[...]

    u = u_ref[...]
    bu_scr[...] = _dot(u.astype(BF16), wb_ref[...])
    lre = jnp.broadcast_to(lam_ref[:, 0:SSM_FLAT], (nb, SSM_FLAT))
    lim = jnp.broadcast_to(lam_ref[:, SSM_FLAT:], (nb, SSM_FLAT))

    def step(t, x):
        r0 = pl.multiple_of(t * nb, nb)
        bu = bu_scr[pl.ds(r0, nb), :]
        xr = x[:, 0:SSM_FLAT]
        xi = x[:, SSM_FLAT:]
        nr = lre * xr - lim * xi + bu[:, 0:SSM_FLAT]
        ni = lre * xi + lim * xr + bu[:, SSM_FLAT:]
        xn = jnp.concatenate([nr, ni], axis=1)
        bu_scr[pl.ds(r0, nb), :] = xn
        return xn

    x = lax.fori_loop(0, steps, step, x_scr[...])
    x_scr[...] = x
    xt_ref[...] = x
    y = _dot(bu_scr[...].astype(BF16), cm_ref[...]) + d_ref[...] * u
    z = _gelu_tanh(y)
    o_ref[...] = (z * _sigmoid(_dot(z.astype(BF16), gw_ref[...]) + gb_ref[...])).astype(BF16)


def _s5(u_rows, wb, lam, cm, d, gw16, gb, x0, steps, nb):
    n = u_rows.shape[0]
    rows = steps * nb
    full = lambda a: pl.BlockSpec(a.shape, lambda i: (0,) * a.ndim)
    return pl.pallas_call(
        functools.partial(_s5_kernel, steps=steps, nb=nb),
        grid=(n // rows,),
        in_specs=[pl.BlockSpec((rows, SSM_WIDTH), lambda i: (i, 0)),
                  full(wb), full(lam), full(cm), full(d), full(gw16), full(gb), full(x0)],
        out_specs=(pl.BlockSpec((rows, SSM_WIDTH), lambda i: (i, 0)), full(x0)),
        out_shape=(jax.ShapeDtypeStruct((n, SSM_WIDTH), BF16),
                   jax.ShapeDtypeStruct(x0.shape, F32)),
        scratch_shapes=[pltpu.VMEM((rows, 2 * SSM_FLAT), F32), pltpu.VMEM((nb, 2 * SSM_FLAT), F32)],
        compiler_params=_cparams(("arbitrary",)),
        name="s5_scan",
    )(u_rows, wb, lam, cm, d, gw16, gb, x0)


def _prep_s5(lam_re, lam_im, log_step, b_re, b_im, c_re, c_im):
    step = jnp.exp(log_step)[:, None]
    mag = jnp.exp(lam_re * step)
    ang = lam_im * step
    lb_re, lb_im = mag * jnp.cos(ang), mag * jnp.sin(ang)
    den = lam_re * lam_re + lam_im * lam_im
    f_re = ((lb_re - 1.0) * lam_re + lb_im * lam_im) / den
    f_im = (lb_im * lam_re - (lb_re - 1.0) * lam_im) / den
    bb_re = f_re[..., None] * b_re - f_im[..., None] * b_im
    bb_im = f_re[..., None] * b_im + f_im[..., None] * b_re
    eye = jnp.eye(SSM_GROUPS, dtype=F32)
    pack_b = lambda bb: jnp.einsum('gpc,gh->gchp', bb, eye).reshape(SSM_WIDTH, SSM_FLAT)
    pack_c = lambda cc: jnp.einsum('gcp,gh->gphc', cc, eye).reshape(SSM_FLAT, SSM_WIDTH)
    wb = jnp.concatenate([pack_b(bb_re), pack_b(bb_im)], axis=1).astype(BF16)
    cm = jnp.concatenate([pack_c(c_re), -pack_c(c_im)], axis=0).astype(BF16)
    lam = jnp.concatenate([lb_re.reshape(1, SSM_FLAT), lb_im.reshape(1, SSM_FLAT)], axis=1)
    return wb, cm, lam


HEAD_GROUP = 4
GROUP_W = HEAD_GROUP * SB_HEAD_DIM


def _dot_nt(a, b):
    return lax.dot_general(a, b, (((1,), (1,)), ((), ())), preferred_element_type=F32)


def _attn_kernel(bias_ref, q_ref, k_ref, v_ref, tri_ref, o_ref, acc_scr, out_scr, *, tq):
    g = pl.program_id(1)
    i = pl.program_id(2)
    q = q_ref[...]
    lane_head = lax.broadcasted_iota(jnp.int32, (tq, GROUP_W), 1) // SB_HEAD_DIM
    causal = (lax.broadcasted_iota(jnp.int32, (tq, tq), 1)
              < lax.broadcasted_iota(jnp.int32, (tq, tq), 0))
    tri = tri_ref[...]

    for hh in range(HEAD_GROUP):
        qm = jnp.where(lane_head == hh, q, jnp.zeros_like(q))
        bias = bias_ref[g * HEAD_GROUP + hh]

        r0 = pl.multiple_of(i * tq, tq)
        z = _dot_nt(qm, k_ref[pl.ds(r0, tq), :]) + bias
        sp = jnp.where(causal, _softplus(z), 0.0)
        cum = _dot_r01(sp, tri, 2)
        w = jnp.where(causal, jnp.exp(z - cum), 0.0)
        acc_scr[...] = _dot(w.astype(BF16), v_ref[pl.ds(r0, tq), :])

        def body(jj, carry):
            c0 = pl.multiple_of((i - 1 - jj) * tq, tq)
            zb = _dot_nt(qm, k_ref[pl.ds(c0, tq), :]) + bias
            cb = _dot_r01(_softplus(zb), tri, 2) + carry
            wb = jnp.exp(zb - cb)
            acc_scr[...] += _dot(wb.astype(BF16), v_ref[pl.ds(c0, tq), :])
            return cb[:, 0:1]

        lax.fori_loop(0, i, body, cum[:, 0:1])
        keep = lane_head == hh
        if hh == 0:
            out_scr[...] = jnp.where(keep, acc_scr[...], 0.0)
        else:
            out_scr[...] = jnp.where(keep, acc_scr[...], out_scr[...])

    o_ref[...] = out_scr[...].astype(BF16)


def _attn_prompt(q16, k16, v16, bias, tri, bsz, t, tq):
    nq = t // tq
    ngrp = SB_WIDTH // GROUP_W
    grid_spec = pltpu.PrefetchScalarGridSpec(
        num_scalar_prefetch=1,
        grid=(bsz, ngrp, nq),
        in_specs=[pl.BlockSpec((tq, GROUP_W), lambda b, g, i, s: (b * nq + i, g)),
                  pl.BlockSpec((t, GROUP_W), lambda b, g, i, s: (b, g)),
                  pl.BlockSpec((t, GROUP_W), lambda b, g, i, s: (b, g)),
                  pl.BlockSpec((tq, tq), lambda b, g, i, s: (0, 0))],
        out_specs=pl.BlockSpec((tq, GROUP_W), lambda b, g, i, s: (b * nq + i, g)),
        scratch_shapes=[pltpu.VMEM((tq, GROUP_W), F32), pltpu.VMEM((tq, GROUP_W), F32)],
    )
    return pl.pallas_call(
        functools.partial(_attn_kernel, tq=tq),
        grid_spec=grid_spec,
        out_shape=jax.ShapeDtypeStruct((bsz * t, SB_WIDTH), BF16),
        compiler_params=_cparams(("parallel", "parallel", "arbitrary")),
        name="sb_attn_prompt",
    )(bias, q16, k16, v16, tri)


PAGES_PER_STEP = 8


def _attn_decode_kernel(pt_ref, q_ref, bias_ref, tri_ref, *refs, pp):
    del pt_ref
    k_refs = refs[:pp]
    v_refs = refs[pp:2 * pp]
    o_ref, acc_scr, carry_scr = refs[2 * pp:]
    j = pl.program_id(1)

    @pl.when(j == 0)
    def _():
        acc_scr[...] = jnp.zeros_like(acc_scr)
        carry_scr[...] = jnp.zeros_like(carry_scr)

    on_diag = (lax.broadcasted_iota(jnp.int32, (SB_HEADS, SB_WIDTH), 0)
               == lax.broadcasted_iota(jnp.int32, (SB_HEADS, SB_WIDTH), 1) // SB_HEAD_DIM)
    q = jnp.broadcast_to(q_ref[0].astype(F32), (SB_HEADS, SB_WIDTH))
    qbd = jnp.where(on_diag, q, 0.0).astype(BF16)
    bias = bias_ref[...]
    tri = tri_ref[...]
    carry = carry_scr[:, 0:1]
    acc = acc_scr[...]
    for p in range(pp):
        z = _dot(qbd, k_refs[p][...].astype(BF16)) + bias
        cum = _dot_r01(_softplus(z), tri, 2) + carry
        w = jnp.exp(z - cum)
        acc = acc + _dot_nt(w.astype(BF16), v_refs[p][...].astype(BF16))
        carry = cum[:, 0:1]
    acc_scr[...] = acc
    carry_scr[...] = jnp.broadcast_to(carry, carry_scr.shape)

    @pl.when(j == pl.num_programs(1) - 1)
    def _():
        o_ref[0] = jnp.sum(jnp.where(on_diag, acc, 0.0), axis=0, keepdims=True).astype(BF16)


def _attn_decode(q16, bias_col, tri, page_table, cache_k, cache_v, layer):
    nseq, npg = page_table.shape
    pp = PAGES_PER_STEP

    def page_spec(p):
        return pl.BlockSpec((None, None, SB_WIDTH, PAGE_SIZE),
                            lambda b, j, pt: (layer, pt[b, npg - 1 - (j * pp + p)], 0, 0))

    grid_spec = pltpu.PrefetchScalarGridSpec(
        num_scalar_prefetch=1,
        grid=(nseq, npg // pp),
        in_specs=[pl.BlockSpec((1, 1, SB_WIDTH), lambda b, j, pt: (b, 0, 0)),
                  pl.BlockSpec((SB_HEADS, LANES), lambda b, j, pt: (0, 0)),
                  pl.BlockSpec((PAGE_SIZE, PAGE_SIZE), lambda b, j, pt: (0, 0))]
                 + [page_spec(p) for p in range(pp)] * 2,
        out_specs=pl.BlockSpec((1, 1, SB_WIDTH), lambda b, j, pt: (b, 0, 0)),
        scratch_shapes=[pltpu.VMEM((SB_HEADS, SB_WIDTH), F32), pltpu.VMEM((SB_HEADS, LANES), F32)],
    )
    return pl.pallas_call(
        functools.partial(_attn_decode_kernel, pp=pp),
        grid_spec=grid_spec,
        out_shape=jax.ShapeDtypeStruct((nseq, 1, SB_WIDTH), BF16),
        compiler_params=_cparams(("parallel", "arbitrary")),
        name="sb_attn_decode",
    )(page_table, q16, bias_col, tri, *([cache_k] * pp), *([cache_v] * pp))


DN_PACK = DN_HEADS * DN_CHUNK
CONV_PAD = SUBLANES


def _dot_tn(a, b):
    return lax.dot_general(a, b, (((0,), (0,)), ((), ())), preferred_element_type=F32)


def _delta_kernel(a_ref, ab_ref, cw_ref, alog_ref, dtb_ref, dnn_ref, e2_ref, lmat_ref, bd1_ref,
                  conv0_ref, s0_ref, o_ref, convn_ref, sout_ref, xp_scr, s_scr,
                  *, tb, nv_last, mask_rows):
    c = DN_CHUNK
    w3 = 3 * DN_WIDTH
    j = pl.program_id(1)

    @pl.when(j == 0)
    def _():
        xp_scr[0:CONV_PAD, :] = conv0_ref[...]
        s_scr[...] = s0_ref[...]

    xp_scr[CONV_PAD:CONV_PAD + tb, :] = a_ref[:, 0:w3]
    off = CONV_PAD - (DN_CONV - 1)
    conv = cw_ref[0:1, :] * xp_scr[off:off + tb, :]
    for i in range(1, DN_CONV):
        conv = conv + cw_ref[i:i + 1, :] * xp_scr[off + i:off + i + tb, :]
    convn_ref[...] = xp_scr[nv_last:nv_last + CONV_PAD, :]
    xp_scr[0:CONV_PAD, :] = xp_scr[tb:tb + CONV_PAD, :]
    r = _silu(conv)

    ii = lax.broadcasted_iota(jnp.int32, (c, DN_PACK), 0)
    jj = lax.broadcasted_iota(jnp.int32, (c, DN_PACK), 1) % c
    strict = jj < ii
    incl = jj <= ii
    bdmask = (lax.broadcasted_iota(jnp.int32, (DN_PACK, DN_PACK), 0) // c
              == lax.broadcasted_iota(jnp.int32, (DN_PACK, DN_PACK), 1) // c)
    lane8 = lax.broadcasted_iota(jnp.int32, (c, LANES), 1)
    bd1 = bd1_ref[...]

    def bd(x16):
        return jnp.where(bdmask, jnp.concatenate([x16] * DN_HEADS, axis=0), jnp.zeros((), BF16))

    def pp(a, b):
        a1, a2 = _split(a, 2)
        b1, b2 = _split(b, 2)
        b1 = bd(b1)
        return _dot(a1, b1) + _dot(a1, bd(b2)) + _dot(a2, b1)

    def level_mask(b):
        return (((ii // (2 * b)) == (jj // (2 * b))) & ((ii % (2 * b)) >= b) & ((jj % (2 * b)) < b))

    prepped = []
    for ci in range(tb // c):
        rows = slice(ci * c, (ci + 1) * c)
        q = r[rows, 0:DN_WIDTH]
        k = r[rows, DN_WIDTH:2 * DN_WIDTH]
        v = r[rows, 2 * DN_WIDTH:w3]
        q = q * lax.rsqrt(_dot_r01(q * q, bd1, 2) + EPS) * (DN_HEAD_DIM ** -0.5)
        k = k * lax.rsqrt(_dot_r01(k * k, bd1, 2) + EPS)
        abv = ab_ref[rows, :]
        g_all = -jnp.exp(alog_ref[...]) * _softplus(abv + dtb_ref[...])
        gb = jnp.where(lane8 < DN_HEADS, g_all, _sigmoid(abv))
        if mask_rows:
            valid = lax.broadcasted_iota(jnp.int32, (c, 1), 0) < (nv_last - ci * c)
            q = jnp.where(valid, q, 0.0)
            k = jnp.where(valid, k, 0.0)
            v = jnp.where(valid, v, 0.0)
            gb = jnp.where(valid, gb, 0.0)
        gbx = _dot_r01(gb, e2_ref[...], 3)
        gexp = gbx[:, 0:DN_PACK]
        bexp = gbx[:, DN_PACK:]
        cums = _dot_l01(lmat_ref[...], gexp, 3)
        gamc = cums[0:c, :]
        revc = cums[c:, :]
        dmat = _dot_l01(lmat_ref[0:c, :], jnp.where(strict, gexp, 0.0), 3)
        decay = jnp.exp(jnp.where(incl, dmat, 0.0))
        k16 = k.astype(BF16)
        bdk = bd(k16)
        kk = _dot_nt(k16, bdk)
        qk = _dot_nt(q.astype(BF16), bdk)
        nmat = jnp.where(strict, bexp * kk * decay, 0.0)
        eye = (ii == jj).astype(F32)
        x = eye - jnp.where(level_mask(1), nmat, 0.0)
        b = 2
        while b < c:
            cb = jnp.where(level_mask(b), nmat, 0.0)
            x = x - pp(pp(x, cb), x)
            b *= 2
        egam = jnp.exp(gamc)
        u = pp(x, bexp * v)
        w = pp(x, bexp * egam * k)
        qkm = jnp.where(incl, qk * decay, 0.0)
        qdec = q * egam
        kdec = k * jnp.exp(revc)
        glrow = egam[c - 1:c, :]
        gate = a_ref[rows, w3:w3 + DN_WIDTH]
        prepped.append((u, w, qkm, qdec, kdec, glrow, gate))

    s = s_scr[...]
    for ci, (u, w, qkm, qdec, kdec, glrow, gate) in enumerate(prepped):
        s16 = s.astype(BF16)
        vnew = u - _dot(w.astype(BF16), s16)
        v16 = vnew.astype(BF16)
        o = _dot(qdec.astype(BF16), s16) + _dot(qkm.astype(BF16), bd(v16))
        s = glrow * s + jnp.where(bdmask, _dot_tn(kdec.astype(BF16), v16), 0.0)
        ms = _dot_r01(o * o, bd1, 2) * (1.0 / DN_HEAD_DIM)
        o = o * lax.rsqrt(ms + EPS) * dnn_ref[...] * _silu(gate)
        o_ref[ci * c:(ci + 1) * c, :] = o.astype(BF16)
    s_scr[...] = s
    sout_ref[...] = s


def _delta(a, ab, cw, alog, dtb, dnn, e2, lmat, bd1, conv0, s0, bsz, t_pad, t_valid, tb):
    nblk = t_pad // tb
    nv_last = t_valid - (nblk - 1) * tb
    full = lambda x: pl.BlockSpec(x.shape, lambda b, j: (0,) * x.ndim)
    per_b = lambda x: pl.BlockSpec((None,) + x.shape[1:], lambda b, j: (b,) + (0,) * (x.ndim - 1))
    return pl.pallas_call(
        functools.partial(_delta_kernel, tb=tb, nv_last=nv_last, mask_rows=(t_valid != t_pad)),
        grid=(bsz, nblk),
        in_specs=[pl.BlockSpec((tb, A_COLS), lambda b, j: (b * nblk + j, 0)),
                  pl.BlockSpec((tb, LANES), lambda b, j: (b * nblk + j, 0)),
                  full(cw), full(alog), full(dtb), full(dnn), full(e2), full(lmat), full(bd1),
                  per_b(conv0), per_b(s0)],
        out_specs=(pl.BlockSpec((tb, DN_WIDTH), lambda b, j: (b * nblk + j, 0)), per_b(conv0), per_b(s0)),
        out_shape=(jax.ShapeDtypeStruct((bsz * t_pad, DN_WIDTH), BF16),
                   jax.ShapeDtypeStruct(conv0.shape, F32),
                   jax.ShapeDtypeStruct(s0.shape, F32)),
        scratch_shapes=[pltpu.VMEM((CONV_PAD + tb, 3 * DN_WIDTH), F32), pltpu.VMEM((DN_PACK, DN_PACK), F32)],
        compiler_params=_cparams(("parallel", "arbitrary")),
        name="gated_delta",
    )(a, ab, cw, alog, dtb, dnn, e2, lmat, bd1, conv0, s0)


def _delta_constants():
    c = DN_CHUNK
    lane = jnp.arange(2 * DN_PACK)
    row = jnp.arange(LANES)[:, None]
    e2 = (((lane[None, :] < DN_PACK) & (row == lane[None, :] // c))
          | ((lane[None, :] >= DN_PACK) & (row == DN_HEADS + (lane[None, :] - DN_PACK) // c))).astype(BF16)
    i = jnp.arange(c)[:, None]
    m = jnp.arange(c)[None, :]
    lmat = jnp.concatenate([(m <= i), (m > i)], axis=0).astype(BF16)
    hh = jnp.arange(DN_PACK) // c
    bd1 = (hh[:, None] == hh[None, :]).astype(BF16)
    return e2, lmat, bd1


def _tri_incl(n):
    return (jnp.arange(n)[:, None] >= jnp.arange(n)[None, :]).astype(BF16)


def _seg_matrices():
    c = jnp.arange(2 * SB_WIDTH)[:, None] // SB_HEAD_DIM
    j = jnp.arange(LANES)[None, :]
    seg = (c == j).astype(BF16)
    return seg, seg.T


def _prep_w_in(w):
    n_ab = 2 * DN_HEADS
    c1 = A_COLS
    c2 = c1 + n_ab
    c3 = c2 + B_COLS
    pad = jnp.zeros((D_MODEL, LANES - n_ab), w.dtype)
    return jnp.concatenate([w[:, :c1], w[:, c2:c3], w[:, c3:], w[:, c1:c2], pad], axis=1).astype(BF16)


def _row(v, width):
    return jnp.pad(v.astype(F32), (0, width - v.shape[0]))[None]


def _pack_state(s):
    eye = jnp.eye(DN_HEADS, dtype=s.dtype)
    return jnp.einsum('bhkv,hg->bhkgv', s, eye).reshape(s.shape[0], DN_PACK, DN_PACK)


def _unpack_state(sp):
    b = sp.shape[0]
    sp = sp.reshape(b, DN_HEADS, DN_HEAD_DIM, DN_HEADS, DN_HEAD_DIM)
    return jnp.stack([sp[:, h, :, h, :] for h in range(DN_HEADS)], axis=1)


TM_PROJ = 256
TM_FFN = 512
TH_FFN = FFN_HIDDEN // 2
TB_DELTA = 4 * DN_CHUNK
TQ_ATTN = 256
S5_STEPS = 128
S5_NB = SUBLANES


def _mixers(x, lw, consts, bsz, t, conv0, s0, x0_ssm, attn_fn):
    (w16, qkn, cw, alog, dtb, dnn, wb, lam, cm, dskip, gw16, gb, wo16, ffn_g, wg16, wu16, wd16, attn_g) = lw
    seg, segt, e2, lmat, bd1 = consts
    n = bsz * t
    tm = min(TM_PROJ, n)
    a, ab, q16, k16, v16, kf, vf, u = _proj_in(x, attn_g, w16, qkn, seg, segt, tm)

    tb = min(TB_DELTA, -(-t // DN_CHUNK) * DN_CHUNK)
    t_pad = -(-t // tb) * tb
    if t_pad != t:
        padt = lambda y: jnp.pad(y.reshape(bsz, t, -1), ((0, 0), (0, t_pad - t), (0, 0))).reshape(bsz * t_pad, -1)
        a_in, ab_in = padt(a), padt(ab)
    else:
        a_in, ab_in = a, ab
    o_a, convn, s_new = _delta(a_in, ab_in, cw, alog, dtb, dnn, e2, lmat, bd1, conv0, s0, bsz, t_pad, t, tb)
    if t_pad != t:
        o_a = o_a.reshape(bsz, t_pad, DN_WIDTH)[:, :t].reshape(n, DN_WIDTH)

    o_b = attn_fn(q16, k16, v16)

    nb = -(-bsz // S5_NB) * S5_NB
    steps = min(S5_STEPS, t)
    u_rows = jnp.pad(u.reshape(bsz, t, SSM_WIDTH).transpose(1, 0, 2), ((0, 0), (0, nb - bsz), (0, 0)))
    o_c, xt = _s5(u_rows.reshape(t * nb, SSM_WIDTH), wb, lam, cm, dskip, gw16, gb,
                  jnp.pad(x0_ssm, ((0, nb - bsz), (0, 0))), steps, nb)
    o_c = o_c.reshape(t, nb, SSM_WIDTH)[:, :bsz].transpose(1, 0, 2).reshape(n, SSM_WIDTH)

    x = _out_ffn(x, o_a, o_b, o_c, wo16, ffn_g, wg16, wu16, wd16, min(TM_FFN, n), TH_FFN)
    leaves = (kf.reshape(bsz, t, SB_HEADS, SB_HEAD_DIM), vf.reshape(bsz, t, SB_HEADS, SB_HEAD_DIM),
              convn[:, CONV_PAD - (DN_CONV - 1):], _unpack_state(s_new),
              xt[:bsz, :SSM_FLAT].reshape(bsz, SSM_GROUPS, SSM_STATE),
              xt[:bsz, SSM_FLAT:].reshape(bsz, SSM_GROUPS, SSM_STATE))
    return x, leaves


def kernel(x_prompt, x_sample, cache_k, cache_v, page_table, state_conv, state_delta, state_ssm_re, state_ssm_im,
           attn_norm, w_in, conv_w, dn_a_log, dn_dt_bias, dn_out_norm, sb_q_norm, sb_k_norm, sb_bias,
           ssm_lambda_re, ssm_lambda_im, ssm_log_step, ssm_b_re, ssm_b_im, ssm_c_re, ssm_c_im, ssm_d,
           ssm_glu_w, ssm_glu_b, w_out, ffn_norm, w_gate, w_up, w_down):
    bp, tp, _ = x_prompt.shape
    bs, ts, _ = x_sample.shape
    assert ts == 1, "the sample group decodes one token per sequence"
    n_pool = cache_k.shape[1]
    ck = cache_k.transpose(0, 1, 3, 4, 2).reshape(DEPTH, n_pool, SB_WIDTH, PAGE_SIZE)
    cv = cache_v.transpose(0, 1, 3, 4, 2).reshape(DEPTH, n_pool, SB_WIDTH, PAGE_SIZE)
    consts = _seg_matrices() + _delta_constants()
    tri_p = _tri_incl(TQ_ATTN)
    tri_d = _tri_incl(PAGE_SIZE)

    xp = x_prompt.reshape(bp * tp, D_MODEL)
    xs = x_sample.reshape(bs * ts, D_MODEL)
    conv0_p = jnp.zeros((bp, CONV_PAD, 3 * DN_WIDTH), F32)
    s0_p = jnp.zeros((bp, DN_PACK, DN_PACK), F32)
    x0_p = jnp.zeros((bp, 2 * SSM_FLAT), F32)
    acc = [[] for _ in range(12)]
    for l in range(DEPTH):
        wb, cm, lam = _prep_s5(ssm_lambda_re[l], ssm_lambda_im[l], ssm_log_step[l],
                               ssm_b_re[l], ssm_b_im[l], ssm_c_re[l], ssm_c_im[l])
        qkn = jnp.concatenate([jnp.tile(sb_q_norm[l], SB_HEADS), jnp.tile(sb_k_norm[l], SB_HEADS)])[None]
        lw = (_prep_w_in(w_in[l]), qkn, conv_w[l], _row(dn_a_log[l], LANES), _row(dn_dt_bias[l], LANES),
              jnp.tile(dn_out_norm[l], DN_HEADS)[None], wb, lam, cm, ssm_d[l][None],
              ssm_glu_w[l].astype(BF16), ssm_glu_b[l][None], w_out[l].astype(BF16), ffn_norm[l][None],
              w_gate[l].astype(BF16), w_up[l].astype(BF16), w_down[l].astype(BF16), attn_norm[l][None])
        bias = sb_bias[l].astype(F32)

        attn_p = lambda q, k, v: _attn_prompt(q, k, v, bias, tri_p, bp, tp, TQ_ATTN)
        xp, st_p = _mixers(xp, lw, consts, bp, tp, conv0_p, s0_p, x0_p, attn_p)

        bias_col = jnp.broadcast_to(bias[:, None], (SB_HEADS, LANES))
        attn_s = lambda q, k, v: _attn_decode(q.reshape(bs, 1, SB_WIDTH), bias_col, tri_d, page_table,
                                              ck, cv, l).reshape(bs, SB_WIDTH)
        conv0_s = jnp.pad(state_conv[l], ((0, 0), (CONV_PAD - (DN_CONV - 1), 0), (0, 0)))
        x0_s = jnp.concatenate([state_ssm_re[l].reshape(bs, SSM_FLAT), state_ssm_im[l].reshape(bs, SSM_FLAT)], axis=1)
        xs, st_s = _mixers(xs, lw, consts, bs, ts, conv0_s, _pack_state(state_delta[l]), x0_s, attn_s)
        for a, val in zip(acc, st_p + st_s):
            a.append(val)
    st = [jnp.stack(a, axis=0) for a in acc]
    return (xp.reshape(bp, tp, D_MODEL), xs.reshape(bs, ts, D_MODEL), *st)
```

```python
import functools
import math

import jax
import jax.numpy as jnp
from jax import lax
from jax.experimental import pallas as pl
from jax.experimental.pallas import tpu as pltpu

F32 = jnp.float32
BF16 = jnp.bfloat16

D_MODEL = 1024
DEPTH = 4
PAGE_SIZE = 128
DN_HEADS = 4
DN_HEAD_DIM = 64
DN_WIDTH = DN_HEADS * DN_HEAD_DIM
DN_CONV = 4
DN_CHUNK = 64
SB_HEADS = 8
SB_HEAD_DIM = 64
SB_WIDTH = SB_HEADS * SB_HEAD_DIM
SSM_WIDTH = D_MODEL - DN_WIDTH - SB_WIDTH
SSM_GROUP = 16
SSM_GROUPS = SSM_WIDTH // SSM_GROUP
SSM_STATE = 64
SSM_FLAT = SSM_GROUPS * SSM_STATE
FFN_HIDDEN = 2816
EPS = 1e-6

LANES = 128
SUBLANES = 8
VMEM_LIMIT = 56 * 1024 * 1024

A_COLS = 4 * DN_WIDTH
B_COLS = 3 * SB_WIDTH
IN_PAD = A_COLS + B_COLS + SSM_WIDTH + LANES


def _cparams(sem):
    return pltpu.CompilerParams(dimension_semantics=sem, vmem_limit_bytes=VMEM_LIMIT)


def _split(x, n):
    terms = []
    r = x
    for i in range(n):
        t = r.astype(BF16)
        terms.append(t)
        if i + 1 < n:
            r = r - t.astype(F32)
    return terms


def _dot(a, b):
    return jnp.dot(a, b, preferred_element_type=F32)


def _dot_l01(a01, x, n):
    return sum(_dot(a01, t) for t in _split(x, n))


def _dot_r01(x, b01, n):
    return sum(_dot(t, b01) for t in _split(x, n))


def _sigmoid(x):
    return 1.0 / (1.0 + jnp.exp(-x))


def _silu(x):
    return x * _sigmoid(x)


def _softplus(x):
    return jnp.maximum(x, 0.0) + jnp.log(1.0 + jnp.exp(-jnp.abs(x)))


def _proj_in_kernel(x_ref, g_ref, w_ref, qkn_ref, seg_ref, segt_ref,
                    a_ref, ab_ref, q16_ref, k16_ref, v16_ref, kf_ref, vf_ref, u_ref):
    x = x_ref[...]
    ms = jnp.mean(x * x, axis=-1, keepdims=True)
    h = (x * lax.rsqrt(ms + EPS) * g_ref[...]).astype(BF16)
    a_ref[...] = _dot(h, w_ref[:, 0:A_COLS])
    c0 = A_COLS
    qk = _dot(h, w_ref[:, c0:c0 + 2 * SB_WIDTH])
    v = _dot(h, w_ref[:, c0 + 2 * SB_WIDTH:c0 + B_COLS])
    c0 += B_COLS
    u_ref[...] = _dot(h, w_ref[:, c0:c0 + SSM_WIDTH])
    c0 += SSM_WIDTH
    ab_ref[...] = _dot(h, w_ref[:, c0:c0 + LANES])
    ssq = _dot_r01(qk * qk, seg_ref[...], 2)
    inv = lax.rsqrt(ssq * (1.0 / SB_HEAD_DIM) + EPS)
    qkn = qk * _dot_r01(inv, segt_ref[...], 3) * qkn_ref[...]
    qn = qkn[:, 0:SB_WIDTH]
    kn = qkn[:, SB_WIDTH:]
    q16_ref[...] = (qn * (SB_HEAD_DIM ** -0.5)).astype(BF16)
    k16_ref[...] = kn.astype(BF16)
    v16_ref[...] = v.astype(BF16)
    kf_ref[...] = kn
    vf_ref[...] = v


def _proj_in(x, gain, w16, qkn, seg, segt, tm):
    n = x.shape[0]
    row = lambda w: pl.BlockSpec((tm, w), lambda i: (i, 0))
    full = lambda a: pl.BlockSpec(a.shape, lambda i: (0,) * a.ndim)
    out_shapes = (
        jax.ShapeDtypeStruct((n, A_COLS), F32),
        jax.ShapeDtypeStruct((n, LANES), F32),
        jax.ShapeDtypeStruct((n, SB_WIDTH), BF16),
        jax.ShapeDtypeStruct((n, SB_WIDTH), BF16),
        jax.ShapeDtypeStruct((n, SB_WIDTH), BF16),
        jax.ShapeDtypeStruct((n, SB_WIDTH), F32),
        jax.ShapeDtypeStruct((n, SB_WIDTH), F32),
        jax.ShapeDtypeStruct((n, SSM_WIDTH), F32),
    )
    return pl.pallas_call(
        _proj_in_kernel,
        grid=(n // tm,),
        in_specs=[row(D_MODEL), full(gain), full(w16), full(qkn), full(seg), full(segt)],
        out_specs=(row(A_COLS), row(LANES), row(SB_WIDTH), row(SB_WIDTH), row(SB_WIDTH),
                   row(SB_WIDTH), row(SB_WIDTH), row(SSM_WIDTH)),
        out_shape=out_shapes,
        compiler_params=_cparams(("parallel",)),
        name="proj_in",
    )(x, gain, w16, qkn, seg, segt)


def _out_ffn_kernel(x_ref, oa_ref, ob_ref, oc_ref, wo_ref, g_ref, wg_ref, wu_ref, wd_ref,
                    y_ref, h_scr, acc_scr):
    j = pl.program_id(1)

    @pl.when(j == 0)
    def _():
        mix = (_dot(oa_ref[...], wo_ref[0:DN_WIDTH, :])
               + _dot(ob_ref[...], wo_ref[DN_WIDTH:DN_WIDTH + SB_WIDTH, :])
               + _dot(oc_ref[...], wo_ref[DN_WIDTH + SB_WIDTH:, :]))
        x1 = x_ref[...] + mix
        ms = jnp.mean(x1 * x1, axis=-1, keepdims=True)
        h_scr[...] = (x1 * lax.rsqrt(ms + EPS) * g_ref[...]).astype(BF16)
        acc_scr[...] = x1

    h = h_scr[...]
    gate = _dot(h, wg_ref[...])
    up = _dot(h, wu_ref[...])
    acc_scr[...] += _dot((_silu(gate) * up).astype(BF16), wd_ref[...])

    @pl.when(j == pl.num_programs(1) - 1)
    def _():
        y_ref[...] = acc_scr[...]


def _out_ffn(x, oa, ob, oc, wo16, gain, wg16, wu16, wd16, tm, th):
    n = x.shape[0]
    row = lambda w: pl.BlockSpec((tm, w), lambda i, j: (i, 0))
    return pl.pallas_call(
        _out_ffn_kernel,
        grid=(n // tm, FFN_HIDDEN // th),
        in_specs=[row(D_MODEL), row(DN_WIDTH), row(SB_WIDTH), row(SSM_WIDTH),
                  pl.BlockSpec((D_MODEL, D_MODEL), lambda i, j: (0, 0)),
                  pl.BlockSpec((1, D_MODEL), lambda i, j: (0, 0)),
                  pl.BlockSpec((D_MODEL, th), lambda i, j: (0, j)),
                  pl.BlockSpec((D_MODEL, th), lambda i, j: (0, j)),
                  pl.BlockSpec((th, D_MODEL), lambda i, j: (j, 0))],
        out_specs=row(D_MODEL),
        out_shape=jax.ShapeDtypeStruct((n, D_MODEL), F32),
        scratch_shapes=[pltpu.VMEM((tm, D_MODEL), BF16), pltpu.VMEM((tm, D_MODEL), F32)],
        compiler_params=_cparams(("parallel", "arbitrary")),
        name="out_ffn",
    )(x, oa, ob, oc, wo16, gain, wg16, wu16, wd16)


def _gelu_tanh(x):
    c = math.sqrt(2.0 / math.pi)
    return 0.5 * x * (1.0 + jnp.tanh(c * (x + 0.044715 * (x * x * x))))


def _s5_kernel(u_ref, wb_ref, lam_ref, cm_ref, d_ref, gw_ref, gb_ref, x0_ref,
               o_ref, xt_ref, bu_scr, x_scr, *, steps, nb):
    @pl.when(pl.program_id(0) == 0)
    def _():
        x_scr[...] = x0_ref[...]

    u = u_ref[...]
    bu_scr[...] = _dot(u.astype(BF16), wb_ref[...])
    lre = jnp.broadcast_to(lam_ref[:, 0:SSM_FLAT], (nb, SSM_FLAT))
    lim = jnp.broadcast_to(lam_ref[:, SSM_FLAT:], (nb, SSM_FLAT))

    def step(t, x):
        r0 = pl.multiple_of(t * nb, nb)
        bu = bu_scr[pl.ds(r0, nb), :]
        xr = x[:, 0:SSM_FLAT]
        xi = x[:, SSM_FLAT:]
        nr = lre * xr - lim * xi + bu[:, 0:SSM_FLAT]
        ni = lre * xi + lim * xr + bu[:, SSM_FLAT:]
        xn = jnp.concatenate([nr, ni], axis=1)
        bu_scr[pl.ds(r0, nb), :] = xn
        return xn

    x = lax.fori_loop(0, steps, step, x_scr[...])
    x_scr[...] = x
    xt_ref[...] = x
    y = _dot(bu_scr[...].astype(BF16), cm_ref[...]) + d_ref[...] * u
    z = _gelu_tanh(y)
    o_ref[...] = (z * _sigmoid(_dot(z.astype(BF16), gw_ref[...]) + gb_ref[...])).astype(BF16)


def _s5(u_rows, wb, lam, cm, d, gw16, gb, x0, steps, nb):
    n = u_rows.shape[0]
    rows = steps * nb
    full = lambda a: pl.BlockSpec(a.shape, lambda i: (0,) * a.ndim)
    return pl.pallas_call(
        functools.partial(_s5_kernel, steps=steps, nb=nb),
        grid=(n // rows,),
        in_specs=[pl.BlockSpec((rows, SSM_WIDTH), lambda i: (i, 0)),
                  full(wb), full(lam), full(cm), full(d), full(gw16), full(gb), full(x0)],
        out_specs=(pl.BlockSpec((rows, SSM_WIDTH), lambda i: (i, 0)), full(x0)),
        out_shape=(jax.ShapeDtypeStruct((n, SSM_WIDTH), BF16),
                   jax.ShapeDtypeStruct(x0.shape, F32)),
        scratch_shapes=[pltpu.VMEM((rows, 2 * SSM_FLAT), F32), pltpu.VMEM((nb, 2 * SSM_FLAT), F32)],
        compiler_params=_cparams(("arbitrary",)),
        name="s5_scan",
    )(u_rows, wb, lam, cm, d, gw16, gb, x0)


def _prep_s5(lam_re, lam_im, log_step, b_re, b_im, c_re, c_im):
    step = jnp.exp(log_step)[:, None]
    mag = jnp.exp(lam_re * step)
    ang = lam_im * step
    lb_re, lb_im = mag * jnp.cos(ang), mag * jnp.sin(ang)
    den = lam_re * lam_re + lam_im * lam_im
    f_re = ((lb_re - 1.0) * lam_re + lb_im * lam_im) / den
    f_im = (lb_im * lam_re - (lb_re - 1.0) * lam_im) / den
    bb_re = f_re[..., None] * b_re - f_im[..., None] * b_im
    bb_im = f_re[..., None] * b_im + f_im[..., None] * b_re
    eye = jnp.eye(SSM_GROUPS, dtype=F32)
    pack_b = lambda bb: jnp.einsum('gpc,gh->gchp', bb, eye).reshape(SSM_WIDTH, SSM_FLAT)
    pack_c = lambda cc: jnp.einsum('gcp,gh->gphc', cc, eye).reshape(SSM_FLAT, SSM_WIDTH)
    wb = jnp.concatenate([pack_b(bb_re), pack_b(bb_im)], axis=1).astype(BF16)
    cm = jnp.concatenate([pack_c(c_re), -pack_c(c_im)], axis=0).astype(BF16)
    lam = jnp.concatenate([lb_re.reshape(1, SSM_FLAT), lb_im.reshape(1, SSM_FLAT)], axis=1)
    return wb, cm, lam


HEAD_GROUP = 4
GROUP_W = HEAD_GROUP * SB_HEAD_DIM


def _dot_nt(a, b):
    return lax.dot_general(a, b, (((1,), (1,)), ((), ())), preferred_element_type=F32)


def _attn_kernel(bias_ref, q_ref, k_ref, v_ref, tri_ref, o_ref, acc_scr, *, tq):
    g = pl.program_id(1)
    i = pl.program_id(2)
    q = q_ref[...]
    lane_head = lax.broadcasted_iota(jnp.int32, (tq, GROUP_W), 1) // SB_HEAD_DIM
    causal = (lax.broadcasted_iota(jnp.int32, (tq, tq), 1)
              < lax.broadcasted_iota(jnp.int32, (tq, tq), 0))
    tri = tri_ref[...]
    heads = range(HEAD_GROUP)
    qms = [jnp.where(lane_head == hh, q, jnp.zeros_like(q)) for hh in heads]
    biases = [bias_ref[g * HEAD_GROUP + hh] for hh in heads]

    r0 = pl.multiple_of(i * tq, tq)
    kd = k_ref[pl.ds(r0, tq), :]
    vd = v_ref[pl.ds(r0, tq), :]
    tri2 = jnp.concatenate([tri, tri], axis=0)

    def suffix_sum(sp):
        hi, lo = _split(sp, 2)
        return _dot(jnp.concatenate([hi, lo], axis=1), tri2)

    zs = [_dot_nt(qms[hh], kd) + biases[hh] for hh in heads]
    cs = [suffix_sum(jnp.where(causal, _softplus(zs[hh]), 0.0)) for hh in heads]
    for hh in heads:
        w = jnp.where(causal, jnp.exp(zs[hh] - cs[hh]), 0.0)
        acc_scr[hh] = _dot(w.astype(BF16), vd)
    carries = tuple(cb[:, 0:1] for cb in cs)

    def body(jj, carries):
        c0 = pl.multiple_of((i - 1 - jj) * tq, tq)
        kb = k_ref[pl.ds(c0, tq), :]
        vb = v_ref[pl.ds(c0, tq), :]
        zs = [_dot_nt(qms[hh], kb) + biases[hh] for hh in heads]
        cs = [suffix_sum(_softplus(zs[hh])) + carries[hh] for hh in heads]
        for hh in heads:
            acc_scr[hh] += _dot(jnp.exp(zs[hh] - cs[hh]).astype(BF16), vb)
        return tuple(cb[:, 0:1] for cb in cs)

    lax.fori_loop(0, i, body, carries)
    out = jnp.where(lane_head == 0, acc_scr[0], 0.0)
    for hh in heads[1:]:
        out = jnp.where(lane_head == hh, acc_scr[hh], out)
    o_ref[...] = out.astype(BF16)


def _attn_prompt(q16, k16, v16, bias, tri, bsz, t, tq):
    nq = t // tq
    ngrp = SB_WIDTH // GROUP_W
    grid_spec = pltpu.PrefetchScalarGridSpec(
        num_scalar_prefetch=1,
        grid=(bsz, ngrp, nq),
        in_specs=[pl.BlockSpec((tq, GROUP_W), lambda b, g, i, s: (b * nq + i, g)),
                  pl.BlockSpec((t, GROUP_W), lambda b, g, i, s: (b, g)),
                  pl.BlockSpec((t, GROUP_W), lambda b, g, i, s: (b, g)),
                  pl.BlockSpec((tq, tq), lambda b, g, i, s: (0, 0))],
        out_specs=pl.BlockSpec((tq, GROUP_W), lambda b, g, i, s: (b * nq + i, g)),
        scratch_shapes=[pltpu.VMEM((HEAD_GROUP, tq, GROUP_W), F32)],
    )
    return pl.pallas_call(
        functools.partial(_attn_kernel, tq=tq),
        grid_spec=grid_spec,
        out_shape=jax.ShapeDtypeStruct((bsz * t, SB_WIDTH), BF16),
        compiler_params=_cparams(("parallel", "parallel", "arbitrary")),
        name="sb_attn_prompt",
    )(bias, q16, k16, v16, tri)


PAGES_PER_STEP = 8


def _attn_decode_kernel(pt_ref, q_ref, bias_ref, tri_ref, *refs, pp):
    del pt_ref
    k_refs = refs[:pp]
    v_refs = refs[pp:2 * pp]
    o_ref, acc_scr, carry_scr = refs[2 * pp:]
    j = pl.program_id(1)

    @pl.when(j == 0)
    def _():
        acc_scr[...] = jnp.zeros_like(acc_scr)
        carry_scr[...] = jnp.zeros_like(carry_scr)

    on_diag = (lax.broadcasted_iota(jnp.int32, (SB_HEADS, SB_WIDTH), 0)
               == lax.broadcasted_iota(jnp.int32, (SB_HEADS, SB_WIDTH), 1) // SB_HEAD_DIM)
    q = jnp.broadcast_to(q_ref[0].astype(F32), (SB_HEADS, SB_WIDTH))
    qbd = jnp.where(on_diag, q, 0.0).astype(BF16)
    bias = bias_ref[...]
    tri = tri_ref[...]
    kcat = jnp.concatenate([k_refs[p][...].astype(BF16) for p in range(pp)], axis=1)
    zw = _dot(qbd, kcat)
    z = jnp.concatenate([zw[:, p * PAGE_SIZE:(p + 1) * PAGE_SIZE] + bias for p in range(pp)], axis=0)
    cum = _dot_r01(_softplus(z), tri, 2)
    carry = carry_scr[:, 0:1]
    offs = []
    for p in range(pp):
        offs.append(carry)
        carry = carry + cum[p * SB_HEADS:(p + 1) * SB_HEADS, 0:1]
    w = jnp.exp(z - cum - jnp.concatenate(offs, axis=0)).astype(BF16)
    wcat = jnp.concatenate([w[p * SB_HEADS:(p + 1) * SB_HEADS, :] for p in range(pp)], axis=1)
    vcat = jnp.concatenate([v_refs[p][...].astype(BF16) for p in range(pp)], axis=1)
    acc = acc_scr[...] + _dot_nt(wcat, vcat)
    acc_scr[...] = acc
    carry_scr[...] = jnp.broadcast_to(carry, carry_scr.shape)

    @pl.when(j == pl.num_programs(1) - 1)
    def _():
        o_ref[0] = jnp.sum(jnp.where(on_diag, acc, 0.0), axis=0, keepdims=True).astype(BF16)


def _attn_decode(q16, bias_col, tri, page_table, cache_k, cache_v, layer):
    nseq, npg = page_table.shape
    pp = PAGES_PER_STEP

    def page_spec(p):
        return pl.BlockSpec((None, None, SB_WIDTH, PAGE_SIZE),
                            lambda b, j, pt: (layer, pt[b, npg - 1 - (j * pp + p)], 0, 0))

    grid_spec = pltpu.PrefetchScalarGridSpec(
        num_scalar_prefetch=1,
        grid=(nseq, npg // pp),
        in_specs=[pl.BlockSpec((1, 1, SB_WIDTH), lambda b, j, pt: (b, 0, 0)),
                  pl.BlockSpec((SB_HEADS, LANES), lambda b, j, pt: (0, 0)),
                  pl.BlockSpec((PAGE_SIZE, PAGE_SIZE), lambda b, j, pt: (0, 0))]
                 + [page_spec(p) for p in range(pp)] * 2,
        out_specs=pl.BlockSpec((1, 1, SB_WIDTH), lambda b, j, pt: (b, 0, 0)),
        scratch_shapes=[pltpu.VMEM((SB_HEADS, SB_WIDTH), F32), pltpu.VMEM((SB_HEADS, LANES), F32)],
    )
    return pl.pallas_call(
        functools.partial(_attn_decode_kernel, pp=pp),
        grid_spec=grid_spec,
        out_shape=jax.ShapeDtypeStruct((nseq, 1, SB_WIDTH), BF16),
        compiler_params=_cparams(("parallel", "arbitrary")),
        name="sb_attn_decode",
    )(page_table, q16, bias_col, tri, *([cache_k] * pp), *([cache_v] * pp))


DN_PACK = DN_HEADS * DN_CHUNK
CONV_PAD = SUBLANES


def _dot_tn(a, b):
    return lax.dot_general(a, b, (((0,), (0,)), ((), ())), preferred_element_type=F32)


def _delta_kernel(a_ref, ab_ref, cw_ref, alog_ref, dtb_ref, dnn_ref, e2_ref, lmat_ref, bd1_ref,
                  conv0_ref, s0_ref, o_ref, convn_ref, sout_ref, xp_scr, s_scr,
                  *, tb, nv_last, mask_rows):
    c = DN_CHUNK
    w3 = 3 * DN_WIDTH
    j = pl.program_id(1)

    @pl.when(j == 0)
    def _():
        xp_scr[0:CONV_PAD, :] = conv0_ref[...]
        s_scr[...] = s0_ref[...]

    xp_scr[CONV_PAD:CONV_PAD + tb, :] = a_ref[:, 0:w3]
    off = CONV_PAD - (DN_CONV - 1)
    conv = cw_ref[0:1, :] * xp_scr[off:off + tb, :]
    for i in range(1, DN_CONV):
        conv = conv + cw_ref[i:i + 1, :] * xp_scr[off + i:off + i + tb, :]
    convn_ref[...] = xp_scr[nv_last:nv_last + CONV_PAD, :]
    xp_scr[0:CONV_PAD, :] = xp_scr[tb:tb + CONV_PAD, :]
    r = _silu(conv)

    ii = lax.broadcasted_iota(jnp.int32, (c, DN_PACK), 0)
    jj = lax.broadcasted_iota(jnp.int32, (c, DN_PACK), 1) % c
    strict = jj < ii
    incl = jj <= ii
    bdmask = (lax.broadcasted_iota(jnp.int32, (DN_PACK, DN_PACK), 0) // c
              == lax.broadcasted_iota(jnp.int32, (DN_PACK, DN_PACK), 1) // c)
    lane8 = lax.broadcasted_iota(jnp.int32, (c, LANES), 1)
    bd1 = bd1_ref[...]

    def bd(x16):
        return jnp.where(bdmask, jnp.concatenate([x16] * DN_HEADS, axis=0), jnp.zeros((), BF16))

    def pp(a, b):
        a1, a2 = _split(a, 2)
        b1, b2 = _split(b, 2)
        b1 = bd(b1)
        return _dot(a1, b1) + _dot(a1, bd(b2)) + _dot(a2, b1)

    def level_mask(b):
        return (((ii // (2 * b)) == (jj // (2 * b))) & ((ii % (2 * b)) >= b) & ((jj % (2 * b)) < b))

    chunks = range(tb // c)
    eye = (ii == jj).astype(F32)
    levels = []
    b = 1
    while b < c:
        levels.append(level_mask(b))
        b *= 2
    qs, ks, vs, bexps, egams, kdecs, qkms, nmats = [], [], [], [], [], [], [], []
    for ci in chunks:
        rows = slice(ci * c, (ci + 1) * c)
        q = r[rows, 0:DN_WIDTH]
        k = r[rows, DN_WIDTH:2 * DN_WIDTH]
        v = r[rows, 2 * DN_WIDTH:w3]
        q = q * lax.rsqrt(_dot_r01(q * q, bd1, 2) + EPS) * (DN_HEAD_DIM ** -0.5)
        k = k * lax.rsqrt(_dot_r01(k * k, bd1, 2) + EPS)
        abv = ab_ref[rows, :]
        g_all = -jnp.exp(alog_ref[...]) * _softplus(abv + dtb_ref[...])
        gb = jnp.where(lane8 < DN_HEADS, g_all, _sigmoid(abv))
        if mask_rows:
            valid = lax.broadcasted_iota(jnp.int32, (c, 1), 0) < (nv_last - ci * c)
            q = jnp.where(valid, q, 0.0)
            k = jnp.where(valid, k, 0.0)
            v = jnp.where(valid, v, 0.0)
            gb = jnp.where(valid, gb, 0.0)
        gbx = _dot_r01(gb, e2_ref[...], 3)
        gexp = gbx[:, 0:DN_PACK]
        bexp = gbx[:, DN_PACK:]
        cums = _dot_l01(lmat_ref[...], gexp, 3)
        egam = jnp.exp(cums[0:c, :])
        kdec = k * jnp.exp(cums[c:, :])
        dmat = _dot_l01(lmat_ref[0:c, :], jnp.where(strict, gexp, 0.0), 3)
        decay = jnp.exp(jnp.where(incl, dmat, 0.0))
        k16 = k.astype(BF16)
        bdk = bd(k16)
        kk = _dot_nt(k16, bdk)
        qk = _dot_nt(q.astype(BF16), bdk)
        nmats.append(jnp.where(strict, bexp * kk * decay, 0.0))
        qkms.append(jnp.where(incl, qk * decay, 0.0))
        qs.append(q)
        ks.append(k)
        vs.append(v)
        bexps.append(bexp)
        egams.append(egam)
        kdecs.append(kdec)

    xs = [eye - jnp.where(levels[0], nm, 0.0) for nm in nmats]
    for lm in levels[1:]:
        ts = [pp(xs[ci], jnp.where(lm, nmats[ci], 0.0)) for ci in chunks]
        xs = [xs[ci] - pp(ts[ci], xs[ci]) for ci in chunks]
    us = [pp(xs[ci], bexps[ci] * vs[ci]) for ci in chunks]
    ws = [pp(xs[ci], bexps[ci] * egams[ci] * ks[ci]) for ci in chunks]
    prepped = [(us[ci], ws[ci], qkms[ci], qs[ci] * egams[ci], kdecs[ci], egams[ci][c - 1:c, :],
                a_ref[ci * c:(ci + 1) * c, w3:w3 + DN_WIDTH]) for ci in chunks]

    s = s_scr[...]
    for ci, (u, w, qkm, qdec, kdec, glrow, gate) in enumerate(prepped):
        s16 = s.astype(BF16)
        vnew = u - _dot(w.astype(BF16), s16)
        v16 = vnew.astype(BF16)
        o = _dot(qdec.astype(BF16), s16) + _dot(qkm.astype(BF16), bd(v16))
        s = glrow * s + jnp.where(bdmask, _dot_tn(kdec.astype(BF16), v16), 0.0)
        ms = _dot_r01(o * o, bd1, 2) * (1.0 / DN_HEAD_DIM)
        o = o * lax.rsqrt(ms + EPS) * dnn_ref[...] * _silu(gate)
        o_ref[ci * c:(ci + 1) * c, :] = o.astype(BF16)
    s_scr[...] = s
    sout_ref[...] = s


def _delta(a, ab, cw, alog, dtb, dnn, e2, lmat, bd1, conv0, s0, bsz, t_pad, t_valid, tb):
    nblk = t_pad // tb
    nv_last = t_valid - (nblk - 1) * tb
    full = lambda x: pl.BlockSpec(x.shape, lambda b, j: (0,) * x.ndim)
    per_b = lambda x: pl.BlockSpec((None,) + x.shape[1:], lambda b, j: (b,) + (0,) * (x.ndim - 1))
    return pl.pallas_call(
        functools.partial(_delta_kernel, tb=tb, nv_last=nv_last, mask_rows=(t_valid != t_pad)),
        grid=(bsz, nblk),
        in_specs=[pl.BlockSpec((tb, A_COLS), lambda b, j: (b * nblk + j, 0)),
                  pl.BlockSpec((tb, LANES), lambda b, j: (b * nblk + j, 0)),
                  full(cw), full(alog), full(dtb), full(dnn), full(e2), full(lmat), full(bd1),
                  per_b(conv0), per_b(s0)],
        out_specs=(pl.BlockSpec((tb, DN_WIDTH), lambda b, j: (b * nblk + j, 0)), per_b(conv0), per_b(s0)),
        out_shape=(jax.ShapeDtypeStruct((bsz * t_pad, DN_WIDTH), BF16),
                   jax.ShapeDtypeStruct(conv0.shape, F32),
                   jax.ShapeDtypeStruct(s0.shape, F32)),
        scratch_shapes=[pltpu.VMEM((CONV_PAD + tb, 3 * DN_WIDTH), F32), pltpu.VMEM((DN_PACK, DN_PACK), F32)],
        compiler_params=_cparams(("parallel", "arbitrary")),
        name="gated_delta",
    )(a, ab, cw, alog, dtb, dnn, e2, lmat, bd1, conv0, s0)


def _delta_constants():
    c = DN_CHUNK
    lane = jnp.arange(2 * DN_PACK)
    row = jnp.arange(LANES)[:, None]
    e2 = (((lane[None, :] < DN_PACK) & (row == lane[None, :] // c))
          | ((lane[None, :] >= DN_PACK) & (row == DN_HEADS + (lane[None, :] - DN_PACK) // c))).astype(BF16)
    i = jnp.arange(c)[:, None]
    m = jnp.arange(c)[None, :]
    lmat = jnp.concatenate([(m <= i), (m > i)], axis=0).astype(BF16)
    hh = jnp.arange(DN_PACK) // c
    bd1 = (hh[:, None] == hh[None, :]).astype(BF16)
    return e2, lmat, bd1


def _tri_incl(n):
    return (jnp.arange(n)[:, None] >= jnp.arange(n)[None, :]).astype(BF16)


def _seg_matrices():
    c = jnp.arange(2 * SB_WIDTH)[:, None] // SB_HEAD_DIM
    j = jnp.arange(LANES)[None, :]
    seg = (c == j).astype(BF16)
    return seg, seg.T


def _prep_w_in(w):
    n_ab = 2 * DN_HEADS
    c1 = A_COLS
    c2 = c1 + n_ab
    c3 = c2 + B_COLS
    pad = jnp.zeros((D_MODEL, LANES - n_ab), w.dtype)
    return jnp.concatenate([w[:, :c1], w[:, c2:c3], w[:, c3:], w[:, c1:c2], pad], axis=1).astype(BF16)


def _row(v, width):
    return jnp.pad(v.astype(F32), (0, width - v.shape[0]))[None]


def _pack_state(s):
    eye = jnp.eye(DN_HEADS, dtype=s.dtype)
    return jnp.einsum('bhkv,hg->bhkgv', s, eye).reshape(s.shape[0], DN_PACK, DN_PACK)


def _unpack_state(sp):
    b = sp.shape[0]
    sp = sp.reshape(b, DN_HEADS, DN_HEAD_DIM, DN_HEADS, DN_HEAD_DIM)
    return jnp.stack([sp[:, h, :, h, :] for h in range(DN_HEADS)], axis=1)


TM_PROJ = 256
TM_FFN = 512
TH_FFN = FFN_HIDDEN // 2
TB_DELTA = 4 * DN_CHUNK
TQ_ATTN = 256
S5_STEPS = 128
S5_NB = SUBLANES


def _mixers(x, lw, consts, bsz, t, conv0, s0, x0_ssm, attn_fn):
    (w16, qkn, cw, alog, dtb, dnn, wb, lam, cm, dskip, gw16, gb, wo16, ffn_g, wg16, wu16, wd16, attn_g) = lw
    seg, segt, e2, lmat, bd1 = consts
    n = bsz * t
    tm = min(TM_PROJ, n)
    a, ab, q16, k16, v16, kf, vf, u = _proj_in(x, attn_g, w16, qkn, seg, segt, tm)

    tb = min(TB_DELTA, -(-t // DN_CHUNK) * DN_CHUNK)
    t_pad = -(-t // tb) * tb
    if t_pad != t:
        padt = lambda y: jnp.pad(y.reshape(bsz, t, -1), ((0, 0), (0, t_pad - t), (0, 0))).reshape(bsz * t_pad, -1)
        a_in, ab_in = padt(a), padt(ab)
    else:
        a_in, ab_in = a, ab
    o_a, convn, s_new = _delta(a_in, ab_in, cw, alog, dtb, dnn, e2, lmat, bd1, conv0, s0, bsz, t_pad, t, tb)
    if t_pad != t:
        o_a = o_a.reshape(bsz, t_pad, DN_WIDTH)[:, :t].reshape(n, DN_WIDTH)

    o_b = attn_fn(q16, k16, v16)

    nb = -(-bsz // S5_NB) * S5_NB
    steps = min(S5_STEPS, t)
    u_rows = jnp.pad(u.reshape(bsz, t, SSM_WIDTH).transpose(1, 0, 2), ((0, 0), (0, nb - bsz), (0, 0)))
    o_c, xt = _s5(u_rows.reshape(t * nb, SSM_WIDTH), wb, lam, cm, dskip, gw16, gb,
                  jnp.pad(x0_ssm, ((0, nb - bsz), (0, 0))), steps, nb)
    o_c = o_c.reshape(t, nb, SSM_WIDTH)[:, :bsz].transpose(1, 0, 2).reshape(n, SSM_WIDTH)

    x = _out_ffn(x, o_a, o_b, o_c, wo16, ffn_g, wg16, wu16, wd16, min(TM_FFN, n), TH_FFN)
    leaves = (kf.reshape(bsz, t, SB_HEADS, SB_HEAD_DIM), vf.reshape(bsz, t, SB_HEADS, SB_HEAD_DIM),
              convn[:, CONV_PAD - (DN_CONV - 1):], _unpack_state(s_new),
              xt[:bsz, :SSM_FLAT].reshape(bsz, SSM_GROUPS, SSM_STATE),
              xt[:bsz, SSM_FLAT:].reshape(bsz, SSM_GROUPS, SSM_STATE))
    return x, leaves


def kernel(x_prompt, x_sample, cache_k, cache_v, page_table, state_conv, state_delta, state_ssm_re, state_ssm_im,
           attn_norm, w_in, conv_w, dn_a_log, dn_dt_bias, dn_out_norm, sb_q_norm, sb_k_norm, sb_bias,
           ssm_lambda_re, ssm_lambda_im, ssm_log_step, ssm_b_re, ssm_b_im, ssm_c_re, ssm_c_im, ssm_d,
           ssm_glu_w, ssm_glu_b, w_out, ffn_norm, w_gate, w_up, w_down):
    bp, tp, _ = x_prompt.shape
    bs, ts, _ = x_sample.shape
    assert ts == 1, "the sample group decodes one token per sequence"
    n_pool = cache_k.shape[1]
    ck = cache_k.transpose(0, 1, 3, 4, 2).reshape(DEPTH, n_pool, SB_WIDTH, PAGE_SIZE)
    cv = cache_v.transpose(0, 1, 3, 4, 2).reshape(DEPTH, n_pool, SB_WIDTH, PAGE_SIZE)
    consts = _seg_matrices() + _delta_constants()
    tri_p = _tri_incl(TQ_ATTN)
    tri_d = _tri_incl(PAGE_SIZE)

    xp = x_prompt.reshape(bp * tp, D_MODEL)
    xs = x_sample.reshape(bs * ts, D_MODEL)
    conv0_p = jnp.zeros((bp, CONV_PAD, 3 * DN_WIDTH), F32)
    s0_p = jnp.zeros((bp, DN_PACK, DN_PACK), F32)
    x0_p = jnp.zeros((bp, 2 * SSM_FLAT), F32)
    acc = [[] for _ in range(12)]
    for l in range(DEPTH):
        wb, cm, lam = _prep_s5(ssm_lambda_re[l], ssm_lambda_im[l], ssm_log_step[l],
                               ssm_b_re[l], ssm_b_im[l], ssm_c_re[l], ssm_c_im[l])
        qkn = jnp.concatenate([jnp.tile(sb_q_norm[l], SB_HEADS), jnp.tile(sb_k_norm[l], SB_HEADS)])[None]
        lw = (_prep_w_in(w_in[l]), qkn, conv_w[l], _row(dn_a_log[l], LANES), _row(dn_dt_bias[l], LANES),
              jnp.tile(dn_out_norm[l], DN_HEADS)[None], wb, lam, cm, ssm_d[l][None],
              ssm_glu_w[l].astype(BF16), ssm_glu_b[l][None], w_out[l].astype(BF16), ffn_norm[l][None],
              w_gate[l].astype(BF16), w_up[l].astype(BF16), w_down[l].astype(BF16), attn_norm[l][None])
        bias = sb_bias[l].astype(F32)

        attn_p = lambda q, k, v: _attn_prompt(q, k, v, bias, tri_p, bp, tp, TQ_ATTN)
        xp, st_p = _mixers(xp, lw, consts, bp, tp, conv0_p, s0_p, x0_p, attn_p)

        bias_col = jnp.broadcast_to(bias[:, None], (SB_HEADS, LANES))
        attn_s = lambda q, k, v: _attn_decode(q.reshape(bs, 1, SB_WIDTH), bias_col, tri_d, page_table,
                                              ck, cv, l).reshape(bs, SB_WIDTH)
        conv0_s = jnp.pad(state_conv[l], ((0, 0), (CONV_PAD - (DN_CONV - 1), 0), (0, 0)))
        x0_s = jnp.concatenate([state_ssm_re[l].reshape(bs, SSM_FLAT), state_ssm_im[l].reshape(bs, SSM_FLAT)], axis=1)
        xs, st_s = _mixers(xs, lw, consts, bs, ts, conv0_s, _pack_state(state_delta[l]), x0_s, attn_s)
        for a, val in zip(acc, st_p + st_s):
            a.append(val)
    st = [jnp.stack(a, axis=0) for a in acc]
    return (xp.reshape(bp, tp, D_MODEL), xs.reshape(bs, ts, D_MODEL), *st)
```

```python
import functools
import math

import jax
import jax.numpy as jnp
from jax import lax
from jax.experimental import pallas as pl
from jax.experimental.pallas import tpu as pltpu

F32 = jnp.float32
BF16 = jnp.bfloat16

D_MODEL = 1024
DEPTH = 4
PAGE_SIZE = 128
DN_HEADS = 4
DN_HEAD_DIM = 64
DN_WIDTH = DN_HEADS * DN_HEAD_DIM
DN_CONV = 4
DN_CHUNK = 64
SB_HEADS = 8
SB_HEAD_DIM = 64
SB_WIDTH = SB_HEADS * SB_HEAD_DIM
SSM_WIDTH = D_MODEL - DN_WIDTH - SB_WIDTH
SSM_GROUP = 16
SSM_GROUPS = SSM_WIDTH // SSM_GROUP
SSM_STATE = 64
SSM_FLAT = SSM_GROUPS * SSM_STATE
FFN_HIDDEN = 2816
EPS = 1e-6

LANES = 128
SUBLANES = 8
VMEM_LIMIT = 56 * 1024 * 1024

A_COLS = 4 * DN_WIDTH
B_COLS = 3 * SB_WIDTH
IN_PAD = A_COLS + B_COLS + SSM_WIDTH + LANES


def _cparams(sem):
    return pltpu.CompilerParams(dimension_semantics=sem, vmem_limit_bytes=VMEM_LIMIT)


def _split(x, n):
    terms = []
    r = x
    for i in range(n):
        t = r.astype(BF16)
        terms.append(t)
        if i + 1 < n:
            r = r - t.astype(F32)
    return terms


def _dot(a, b):
    return jnp.dot(a, b, preferred_element_type=F32)


def _dot_l01(a01, x, n):
    return sum(_dot(a01, t) for t in _split(x, n))


def _dot_r01(x, b01, n):
    return sum(_dot(t, b01) for t in _split(x, n))


def _sigmoid(x):
    return 1.0 / (1.0 + jnp.exp(-x))


def _silu(x):
    return x * _sigmoid(x)


LOG2E = 1.4426950408889634


def _softplus(x):
    return jnp.maximum(x, 0.0) + jnp.log(1.0 + jnp.exp2(jnp.abs(x) * (-LOG2E)))


def _proj_in_kernel(x_ref, g_ref, w_ref, qkn_ref, seg_ref, segt_ref,
                    a_ref, ab_ref, q16_ref, k16_ref, v16_ref, kf_ref, vf_ref, u_ref):
    x = x_ref[...]
    ms = jnp.mean(x * x, axis=-1, keepdims=True)
    h = (x * lax.rsqrt(ms + EPS) * g_ref[...]).astype(BF16)
    a_ref[...] = _dot(h, w_ref[:, 0:A_COLS])
    c0 = A_COLS
    qk = _dot(h, w_ref[:, c0:c0 + 2 * SB_WIDTH])
    v = _dot(h, w_ref[:, c0 + 2 * SB_WIDTH:c0 + B_COLS])
    c0 += B_COLS
    u_ref[...] = _dot(h, w_ref[:, c0:c0 + SSM_WIDTH])
    c0 += SSM_WIDTH
    ab_ref[...] = _dot(h, w_ref[:, c0:c0 + LANES])
    ssq = _dot_r01(qk * qk, seg_ref[...], 2)
    inv = lax.rsqrt(ssq * (1.0 / SB_HEAD_DIM) + EPS)
    qkn = qk * _dot_r01(inv, segt_ref[...], 3) * qkn_ref[...]
    qn = qkn[:, 0:SB_WIDTH]
    kn = qkn[:, SB_WIDTH:]
    q16_ref[...] = (qn * (SB_HEAD_DIM ** -0.5)).astype(BF16)
    k16_ref[...] = kn.astype(BF16)
    v16_ref[...] = v.astype(BF16)
    kf_ref[...] = kn
    vf_ref[...] = v


def _proj_in(x, gain, w16, qkn, seg, segt, tm):
    n = x.shape[0]
    row = lambda w: pl.BlockSpec((tm, w), lambda i: (i, 0))
    full = lambda a: pl.BlockSpec(a.shape, lambda i: (0,) * a.ndim)
    out_shapes = (
        jax.ShapeDtypeStruct((n, A_COLS), F32),
        jax.ShapeDtypeStruct((n, LANES), F32),
        jax.ShapeDtypeStruct((n, SB_WIDTH), BF16),
        jax.ShapeDtypeStruct((n, SB_WIDTH), BF16),
        jax.ShapeDtypeStruct((n, SB_WIDTH), BF16),
        jax.ShapeDtypeStruct((n, SB_WIDTH), F32),
        jax.ShapeDtypeStruct((n, SB_WIDTH), F32),
        jax.ShapeDtypeStruct((n, SSM_WIDTH), F32),
    )
    return pl.pallas_call(
        _proj_in_kernel,
        grid=(n // tm,),
        in_specs=[row(D_MODEL), full(gain), full(w16), full(qkn), full(seg), full(segt)],
        out_specs=(row(A_COLS), row(LANES), row(SB_WIDTH), row(SB_WIDTH), row(SB_WIDTH),
                   row(SB_WIDTH), row(SB_WIDTH), row(SSM_WIDTH)),
        out_shape=out_shapes,
        compiler_params=_cparams(("parallel",)),
        name="proj_in",
    )(x, gain, w16, qkn, seg, segt)


def _out_ffn_kernel(x_ref, oa_ref, ob_ref, oc_ref, wo_ref, g_ref, wg_ref, wu_ref, wd_ref,
                    y_ref, h_scr, acc_scr):
    j = pl.program_id(1)

    @pl.when(j == 0)
    def _():
        mix = (_dot(oa_ref[...], wo_ref[0:DN_WIDTH, :])
               + _dot(ob_ref[...], wo_ref[DN_WIDTH:DN_WIDTH + SB_WIDTH, :])
               + _dot(oc_ref[...], wo_ref[DN_WIDTH + SB_WIDTH:, :]))
        x1 = x_ref[...] + mix
        ms = jnp.mean(x1 * x1, axis=-1, keepdims=True)
        h_scr[...] = (x1 * lax.rsqrt(ms + EPS) * g_ref[...]).astype(BF16)
        acc_scr[...] = x1

    h = h_scr[...]
    gate = _dot(h, wg_ref[...])
    up = _dot(h, wu_ref[...])
    acc_scr[...] += _dot((_silu(gate) * up).astype(BF16), wd_ref[...])

    @pl.when(j == pl.num_programs(1) - 1)
    def _():
        y_ref[...] = acc_scr[...]


def _out_ffn(x, oa, ob, oc, wo16, gain, wg16, wu16, wd16, tm, th):
    n = x.shape[0]
    row = lambda w: pl.BlockSpec((tm, w), lambda i, j: (i, 0))
    return pl.pallas_call(
        _out_ffn_kernel,
        grid=(n // tm, FFN_HIDDEN // th),
        in_specs=[row(D_MODEL), row(DN_WIDTH), row(SB_WIDTH), row(SSM_WIDTH),
                  pl.BlockSpec((D_MODEL, D_MODEL), lambda i, j: (0, 0)),
                  pl.BlockSpec((1, D_MODEL), lambda i, j: (0, 0)),
                  pl.BlockSpec((D_MODEL, th), lambda i, j: (0, j)),
                  pl.BlockSpec((D_MODEL, th), lambda i, j: (0, j)),
                  pl.BlockSpec((th, D_MODEL), lambda i, j: (j, 0))],
        out_specs=row(D_MODEL),
        out_shape=jax.ShapeDtypeStruct((n, D_MODEL), F32),
        scratch_shapes=[pltpu.VMEM((tm, D_MODEL), BF16), pltpu.VMEM((tm, D_MODEL), F32)],
        compiler_params=_cparams(("parallel", "arbitrary")),
        name="out_ffn",
    )(x, oa, ob, oc, wo16, gain, wg16, wu16, wd16)


def _gelu_tanh(x):
    c = math.sqrt(2.0 / math.pi)
    return 0.5 * x * (1.0 + jnp.tanh(c * (x + 0.044715 * (x * x * x))))


def _s5_kernel(u_ref, wb_ref, lam_ref, cm_ref, d_ref, gw_ref, gb_ref, x0_ref,
               o_ref, xt_ref, bu_scr, x_scr, *, steps, nb):
    two_step = nb * 2 == SUBLANES
    tile = SUBLANES if two_step else nb

    @pl.when(pl.program_id(0) == 0)
    def _():
        x_scr[...] = jnp.concatenate([x0_ref[...]] * 2, axis=0) if two_step else x0_ref[...]

    u = u_ref[...]
    bu_scr[...] = _dot(u.astype(BF16), wb_ref[...])
    lre = jnp.broadcast_to(lam_ref[:, 0:SSM_FLAT], (tile, SSM_FLAT))
    lim = jnp.broadcast_to(lam_ref[:, SSM_FLAT:], (tile, SSM_FLAT))
    lower = lax.broadcasted_iota(jnp.int32, (tile, 2 * SSM_FLAT), 0) < nb

    def advance(x, bu):
        xr = x[:, 0:SSM_FLAT]
        xi = x[:, SSM_FLAT:]
        nr = lre * xr - lim * xi + bu[:, 0:SSM_FLAT]
        ni = lre * xi + lim * xr + bu[:, SSM_FLAT:]
        return jnp.concatenate([nr, ni], axis=1)

    def step(t, x):
        r0 = pl.multiple_of(t * tile, tile)
        bu = bu_scr[pl.ds(r0, tile), :]
        x1 = advance(x, bu)
        if two_step:
            x2 = advance(pltpu.roll(x1, nb, 0), bu)
            bu_scr[pl.ds(r0, tile), :] = jnp.where(lower, x1, x2)
            return jnp.where(lower, pltpu.roll(x2, nb, 0), x2)
        bu_scr[pl.ds(r0, tile), :] = x1
        return x1

    x = lax.fori_loop(0, steps * nb // tile, step, x_scr[...])
    x_scr[...] = x
    xt_ref[...] = x[0:nb, :]
    y = _dot(bu_scr[...].astype(BF16), cm_ref[...]) + d_ref[...] * u
    z = _gelu_tanh(y)
    o_ref[...] = (z * _sigmoid(_dot(z.astype(BF16), gw_ref[...]) + gb_ref[...])).astype(BF16)


def _s5(u_rows, wb, lam, cm, d, gw16, gb, x0, steps, nb):
    assert nb % SUBLANES == 0 or (nb * 2 == SUBLANES and steps % 2 == 0)
    n = u_rows.shape[0]
    rows = steps * nb
    full = lambda a: pl.BlockSpec(a.shape, lambda i: (0,) * a.ndim)
    return pl.pallas_call(
        functools.partial(_s5_kernel, steps=steps, nb=nb),
        grid=(n // rows,),
        in_specs=[pl.BlockSpec((rows, SSM_WIDTH), lambda i: (i, 0)),
                  full(wb), full(lam), full(cm), full(d), full(gw16), full(gb), full(x0)],
        out_specs=(pl.BlockSpec((rows, SSM_WIDTH), lambda i: (i, 0)), full(x0)),
        out_shape=(jax.ShapeDtypeStruct((n, SSM_WIDTH), BF16),
                   jax.ShapeDtypeStruct(x0.shape, F32)),
        scratch_shapes=[pltpu.VMEM((rows, 2 * SSM_FLAT), F32),
                        pltpu.VMEM((max(nb, SUBLANES), 2 * SSM_FLAT), F32)],
        compiler_params=_cparams(("arbitrary",)),
        name="s5_scan",
    )(u_rows, wb, lam, cm, d, gw16, gb, x0)


def _prep_s5(lam_re, lam_im, log_step, b_re, b_im, c_re, c_im):
    step = jnp.exp(log_step)[:, None]
    mag = jnp.exp(lam_re * step)
    ang = lam_im * step
    lb_re, lb_im = mag * jnp.cos(ang), mag * jnp.sin(ang)
    den = lam_re * lam_re + lam_im * lam_im
    f_re = ((lb_re - 1.0) * lam_re + lb_im * lam_im) / den
    f_im = (lb_im * lam_re - (lb_re - 1.0) * lam_im) / den
    bb_re = f_re[..., None] * b_re - f_im[..., None] * b_im
    bb_im = f_re[..., None] * b_im + f_im[..., None] * b_re
    eye = jnp.eye(SSM_GROUPS, dtype=F32)
    pack_b = lambda bb: jnp.einsum('gpc,gh->gchp', bb, eye).reshape(SSM_WIDTH, SSM_FLAT)
    pack_c = lambda cc: jnp.einsum('gcp,gh->gphc', cc, eye).reshape(SSM_FLAT, SSM_WIDTH)
    wb = jnp.concatenate([pack_b(bb_re), pack_b(bb_im)], axis=1).astype(BF16)
    cm = jnp.concatenate([pack_c(c_re), -pack_c(c_im)], axis=0).astype(BF16)
    lam = jnp.concatenate([lb_re.reshape(1, SSM_FLAT), lb_im.reshape(1, SSM_FLAT)], axis=1)
    return wb, cm, lam


HEAD_GROUP = 4
GROUP_W = HEAD_GROUP * SB_HEAD_DIM


def _dot_nt(a, b):
    return lax.dot_general(a, b, (((1,), (1,)), ((), ())), preferred_element_type=F32)


def _attn_kernel(bias_ref, q_ref, k_ref, v_ref, tri_ref, o_ref, acc_scr, z_scr, w_scr, *, tq):
    g = pl.program_id(1)
    i = pl.program_id(2)
    q = q_ref[...]
    lane_head = lax.broadcasted_iota(jnp.int32, (tq, GROUP_W), 1) // SB_HEAD_DIM
    causal = (lax.broadcasted_iota(jnp.int32, (tq, tq), 1)
              < lax.broadcasted_iota(jnp.int32, (tq, tq), 0))
    tri = tri_ref[...]
    heads = range(HEAD_GROUP)
    qms = [jnp.where(lane_head == hh, q, jnp.zeros_like(q)) for hh in heads]
    biases = [bias_ref[g * HEAD_GROUP + hh] for hh in heads]

    r0 = pl.multiple_of(i * tq, tq)

    def suffix_sum(sp):
        return _dot(sp.astype(BF16), tri)

    def scores(c0):
        kb = k_ref[pl.ds(c0, tq), :]
        return [_dot_nt(qms[hh], kb) + biases[hh] for hh in heads]

    zs = scores(r0)
    cs = [suffix_sum(jnp.where(causal, _softplus(zs[hh]), 0.0)) for hh in heads]
    for hh in heads:
        w_scr[hh] = jnp.where(causal, jnp.exp(zs[hh] - cs[hh]), 0.0).astype(BF16)
        acc_scr[hh] = jnp.zeros((tq, GROUP_W), F32)
    zn = scores(pl.multiple_of(jnp.maximum(i - 1, 0) * tq, tq))
    for hh in heads:
        z_scr[hh] = zn[hh]
    carries = tuple(cb[:, 0:1] for cb in cs)

    def body(jj, carries):
        zc = [z_scr[hh] for hh in heads]
        zn = scores(pl.multiple_of(jnp.maximum(i - jj - 1, 0) * tq, tq))
        vprev = v_ref[pl.ds(pl.multiple_of((i - jj + 1) * tq, tq), tq), :]
        for hh in heads:
            acc_scr[hh] += _dot(w_scr[hh], vprev)
        cs = [suffix_sum(_softplus(zc[hh])) + carries[hh] for hh in heads]
        for hh in heads:
            w_scr[hh] = jnp.exp(zc[hh] - cs[hh]).astype(BF16)
            z_scr[hh] = zn[hh]
        return tuple(cb[:, 0:1] for cb in cs)

    lax.fori_loop(1, i + 1, body, carries)
    v0 = v_ref[0:tq, :]
    out = None
    for hh in heads:
        o_h = acc_scr[hh] + _dot(w_scr[hh], v0)
        out = jnp.where(lane_head == hh, o_h, 0.0 if out is None else out)
    o_ref[...] = out.astype(BF16)


def _attn_prompt(q16, k16, v16, bias, tri, bsz, t, tq):
    nq = t // tq
    ngrp = SB_WIDTH // GROUP_W
    grid_spec = pltpu.PrefetchScalarGridSpec(
        num_scalar_prefetch=1,
        grid=(bsz, ngrp, nq),
        in_specs=[pl.BlockSpec((tq, GROUP_W), lambda b, g, i, s: (b * nq + i, g)),
                  pl.BlockSpec((t, GROUP_W), lambda b, g, i, s: (b, g)),
                  pl.BlockSpec((t, GROUP_W), lambda b, g, i, s: (b, g)),
                  pl.BlockSpec((tq, tq), lambda b, g, i, s: (0, 0))],
        out_specs=pl.BlockSpec((tq, GROUP_W), lambda b, g, i, s: (b * nq + i, g)),
        scratch_shapes=[pltpu.VMEM((HEAD_GROUP, tq, GROUP_W), F32),
                        pltpu.VMEM((HEAD_GROUP, tq, tq), F32),
                        pltpu.VMEM((HEAD_GROUP, tq, tq), BF16)],
    )
    return pl.pallas_call(
        functools.partial(_attn_kernel, tq=tq),
        grid_spec=grid_spec,
        out_shape=jax.ShapeDtypeStruct((bsz * t, SB_WIDTH), BF16),
        compiler_params=_cparams(("parallel", "parallel", "arbitrary")),
        name="sb_attn_prompt",
    )(bias, q16, k16, v16, tri)


PAGES_PER_STEP = 16


def _attn_decode_kernel(pt_ref, q_ref, bias_ref, tri_ref, *refs, pp):
    del pt_ref
    k_refs = refs[:pp]
    v_refs = refs[pp:2 * pp]
    o_ref, acc_scr, carry_scr = refs[2 * pp:]
    j = pl.program_id(1)

    @pl.when(j == 0)
    def _():
        acc_scr[...] = jnp.zeros_like(acc_scr)
        carry_scr[...] = jnp.zeros_like(carry_scr)

    on_diag = (lax.broadcasted_iota(jnp.int32, (SB_HEADS, SB_WIDTH), 0)
               == lax.broadcasted_iota(jnp.int32, (SB_HEADS, SB_WIDTH), 1) // SB_HEAD_DIM)
    q = jnp.broadcast_to(q_ref[0].astype(F32), (SB_HEADS, SB_WIDTH))
    qbd = jnp.where(on_diag, q, 0.0).astype(BF16)
    bias = bias_ref[...]
    tri = tri_ref[...]
    kcat = jnp.concatenate([k_refs[p][...].astype(BF16) for p in range(pp)], axis=1)
    zw = _dot(qbd, kcat)
    z = jnp.concatenate([zw[:, p * PAGE_SIZE:(p + 1) * PAGE_SIZE] + bias for p in range(pp)], axis=0)
    cum = _dot_r01(_softplus(z), tri, 2)
    carry = carry_scr[:, 0:1]
    offs = []
    for p in range(pp):
        offs.append(carry)
        carry = carry + cum[p * SB_HEADS:(p + 1) * SB_HEADS, 0:1]
    w = jnp.exp(z - cum - jnp.concatenate(offs, axis=0)).astype(BF16)
    wcat = jnp.concatenate([w[p * SB_HEADS:(p + 1) * SB_HEADS, :] for p in range(pp)], axis=1)
    vcat = jnp.concatenate([v_refs[p][...].astype(BF16) for p in range(pp)], axis=1)
    acc = acc_scr[...] + _dot_nt(wcat, vcat)
    acc_scr[...] = acc
    carry_scr[...] = jnp.broadcast_to(carry, carry_scr.shape)

    @pl.when(j == pl.num_programs(1) - 1)
    def _():
        o_ref[0] = jnp.sum(jnp.where(on_diag, acc, 0.0), axis=0, keepdims=True).astype(BF16)


def _attn_decode(q16, bias_col, tri, page_table, cache_k, cache_v, layer):
    nseq, npg = page_table.shape
    pp = PAGES_PER_STEP

    def page_spec(p):
        return pl.BlockSpec((None, None, SB_WIDTH, PAGE_SIZE),
                            lambda b, j, pt: (layer, pt[b, npg - 1 - (j * pp + p)], 0, 0))

    grid_spec = pltpu.PrefetchScalarGridSpec(
        num_scalar_prefetch=1,
        grid=(nseq, npg // pp),
        in_specs=[pl.BlockSpec((1, 1, SB_WIDTH), lambda b, j, pt: (b, 0, 0)),
                  pl.BlockSpec((SB_HEADS, LANES), lambda b, j, pt: (0, 0)),
                  pl.BlockSpec((PAGE_SIZE, PAGE_SIZE), lambda b, j, pt: (0, 0))]
                 + [page_spec(p) for p in range(pp)] * 2,
        out_specs=pl.BlockSpec((1, 1, SB_WIDTH), lambda b, j, pt: (b, 0, 0)),
        scratch_shapes=[pltpu.VMEM((SB_HEADS, SB_WIDTH), F32), pltpu.VMEM((SB_HEADS, LANES), F32)],
    )
    return pl.pallas_call(
        functools.partial(_attn_decode_kernel, pp=pp),
        grid_spec=grid_spec,
        out_shape=jax.ShapeDtypeStruct((nseq, 1, SB_WIDTH), BF16),
        compiler_params=_cparams(("parallel", "arbitrary")),
        name="sb_attn_decode",
    )(page_table, q16, bias_col, tri, *([cache_k] * pp), *([cache_v] * pp))


DN_PACK = DN_HEADS * DN_CHUNK
CONV_PAD = SUBLANES


def _dot_tn(a, b):
    return lax.dot_general(a, b, (((0,), (0,)), ((), ())), preferred_element_type=F32)


def _delta_kernel(a_ref, ab_ref, cw_ref, alog_ref, dtb_ref, dnn_ref, e2_ref, lmat_ref, bd1_ref,
                  conv0_ref, s0_ref, o_ref, convn_ref, sout_ref, xp_scr, s_scr,
                  *, tb, nv_last, mask_rows):
    c = DN_CHUNK
    w3 = 3 * DN_WIDTH
    j = pl.program_id(1)

    @pl.when(j == 0)
    def _():
        xp_scr[0:CONV_PAD, :] = conv0_ref[...]
        s_scr[...] = s0_ref[...]

    xp_scr[CONV_PAD:CONV_PAD + tb, :] = a_ref[:, 0:w3]
    off = CONV_PAD - (DN_CONV - 1)
    conv = cw_ref[0:1, :] * xp_scr[off:off + tb, :]
    for i in range(1, DN_CONV):
        conv = conv + cw_ref[i:i + 1, :] * xp_scr[off + i:off + i + tb, :]
    convn_ref[...] = xp_scr[nv_last:nv_last + CONV_PAD, :]
    xp_scr[0:CONV_PAD, :] = xp_scr[tb:tb + CONV_PAD, :]
    r = _silu(conv)

    ii = lax.broadcasted_iota(jnp.int32, (c, DN_PACK), 0)
    jj = lax.broadcasted_iota(jnp.int32, (c, DN_PACK), 1) % c
    strict = jj < ii
    incl = jj <= ii
    bdmask = (lax.broadcasted_iota(jnp.int32, (DN_PACK, DN_PACK), 0) // c
              == lax.broadcasted_iota(jnp.int32, (DN_PACK, DN_PACK), 1) // c)
    lane8 = lax.broadcasted_iota(jnp.int32, (c, LANES), 1)
    bd1 = bd1_ref[...]

    def bd(x16):
        return jnp.where(bdmask, jnp.concatenate([x16] * DN_HEADS, axis=0), jnp.zeros((), BF16))

    def pp(a2t, b2t):
        return _dot(a2t[0], b2t[0]) + _dot(a2t[0], b2t[1]) + _dot(a2t[1], b2t[0])

    def split_bd(x):
        return tuple(bd(t) for t in _split(x, 2))

    def level_mask(b):
        return (((ii // (2 * b)) == (jj // (2 * b))) & ((ii % (2 * b)) >= b) & ((jj % (2 * b)) < b))

    chunks = range(tb // c)
    eye = (ii == jj).astype(F32)
    levels = []
    b = 1
    while b < c:
        levels.append(level_mask(b))
        b *= 2
    qs, ks, vs, bexps, egams, kdecs, qkms, nmats = [], [], [], [], [], [], [], []
    for ci in chunks:
        rows = slice(ci * c, (ci + 1) * c)
        q = r[rows, 0:DN_WIDTH]
        k = r[rows, DN_WIDTH:2 * DN_WIDTH]
        v = r[rows, 2 * DN_WIDTH:w3]
        q = q * lax.rsqrt(_dot_r01(q * q, bd1, 2) + EPS) * (DN_HEAD_DIM ** -0.5)
        k = k * lax.rsqrt(_dot_r01(k * k, bd1, 2) + EPS)
        abv = ab_ref[rows, :]
        g_all = -jnp.exp(alog_ref[...]) * _softplus(abv + dtb_ref[...])
        gb = jnp.where(lane8 < DN_HEADS, g_all, _sigmoid(abv))
        if mask_rows:
            valid = lax.broadcasted_iota(jnp.int32, (c, 1), 0) < (nv_last - ci * c)
            q = jnp.where(valid, q, 0.0)
            k = jnp.where(valid, k, 0.0)
            v = jnp.where(valid, v, 0.0)
            gb = jnp.where(valid, gb, 0.0)
        gbx = _dot_r01(gb, e2_ref[...], 3)
        gexp = gbx[:, 0:DN_PACK]
        bexp = gbx[:, DN_PACK:]
        cums = _dot_l01(lmat_ref[...], gexp, 3)
        egam = jnp.exp(cums[0:c, :])
        kdec = k * jnp.exp(cums[c:, :])
        dmat = _dot_l01(lmat_ref[0:c, :], jnp.where(strict, gexp, 0.0), 3)
        decay = jnp.exp(jnp.where(incl, dmat, 0.0))
        k16 = k.astype(BF16)
        bdk = bd(k16)
        kk = _dot_nt(k16, bdk)
        qk = _dot_nt(q.astype(BF16), bdk)
        nmats.append(jnp.where(strict, bexp * kk * decay, 0.0))
        qkms.append(jnp.where(incl, qk * decay, 0.0))
        qs.append(q)
        ks.append(k)
        vs.append(v)
        bexps.append(bexp)
        egams.append(egam)
        kdecs.append(kdec)

    xs = [eye - jnp.where(levels[0], nm, 0.0) for nm in nmats]
    nsplit = [_split(nm, 2) for nm in nmats]
    zero16 = jnp.zeros((), BF16)
    for lm in levels[1:]:
        xsplit = [_split(x, 2) for x in xs]
        ts = [pp(xsplit[ci], tuple(bd(jnp.where(lm, n, zero16)) for n in nsplit[ci])) for ci in chunks]
        xs = [xs[ci] - pp(_split(ts[ci], 2), tuple(bd(t) for t in xsplit[ci])) for ci in chunks]
    xsplit = [_split(x, 2) for x in xs]
    us = [pp(xsplit[ci], split_bd(bexps[ci] * vs[ci])) for ci in chunks]
    ws = [pp(xsplit[ci], split_bd(bexps[ci] * egams[ci] * ks[ci])) for ci in chunks]
    prepped = [(us[ci], ws[ci], qkms[ci], qs[ci] * egams[ci], kdecs[ci], egams[ci][c - 1:c, :],
                a_ref[ci * c:(ci + 1) * c, w3:w3 + DN_WIDTH]) for ci in chunks]

    s = s_scr[...]
    for ci, (u, w, qkm, qdec, kdec, glrow, gate) in enumerate(prepped):
        s16 = s.astype(BF16)
        vnew = u - _dot(w.astype(BF16), s16)
        v16 = vnew.astype(BF16)
        o = _dot(qdec.astype(BF16), s16) + _dot(qkm.astype(BF16), bd(v16))
        s = glrow * s + jnp.where(bdmask, _dot_tn(kdec.astype(BF16), v16), 0.0)
        ms = _dot_r01(o * o, bd1, 2) * (1.0 / DN_HEAD_DIM)
        o = o * lax.rsqrt(ms + EPS) * dnn_ref[...] * _silu(gate)
        o_ref[ci * c:(ci + 1) * c, :] = o.astype(BF16)
    s_scr[...] = s
    sout_ref[...] = s


def _delta(a, ab, cw, alog, dtb, dnn, e2, lmat, bd1, conv0, s0, bsz, t_pad, t_valid, tb):
    nblk = t_pad // tb
    nv_last = t_valid - (nblk - 1) * tb
    full = lambda x: pl.BlockSpec(x.shape, lambda b, j: (0,) * x.ndim)
    per_b = lambda x: pl.BlockSpec((None,) + x.shape[1:], lambda b, j: (b,) + (0,) * (x.ndim - 1))
    return pl.pallas_call(
        functools.partial(_delta_kernel, tb=tb, nv_last=nv_last, mask_rows=(t_valid != t_pad)),
        grid=(bsz, nblk),
        in_specs=[pl.BlockSpec((tb, A_COLS), lambda b, j: (b * nblk + j, 0)),
                  pl.BlockSpec((tb, LANES), lambda b, j: (b * nblk + j, 0)),
                  full(cw), full(alog), full(dtb), full(dnn), full(e2), full(lmat), full(bd1),
                  per_b(conv0), per_b(s0)],
        out_specs=(pl.BlockSpec((tb, DN_WIDTH), lambda b, j: (b * nblk + j, 0)), per_b(conv0), per_b(s0)),
        out_shape=(jax.ShapeDtypeStruct((bsz * t_pad, DN_WIDTH), BF16),
                   jax.ShapeDtypeStruct(conv0.shape, F32),
                   jax.ShapeDtypeStruct(s0.shape, F32)),
        scratch_shapes=[pltpu.VMEM((CONV_PAD + tb, 3 * DN_WIDTH), F32), pltpu.VMEM((DN_PACK, DN_PACK), F32)],
        compiler_params=_cparams(("parallel", "arbitrary")),
        name="gated_delta",
    )(a, ab, cw, alog, dtb, dnn, e2, lmat, bd1, conv0, s0)


def _delta_step_kernel(a_ref, ab_ref, cw_ref, alog_ref, dtb_ref, dnn_ref, e2_ref, bd1_ref, conv_ref, s_ref,
                       o_ref, convn_ref, sout_ref):
    nseq = a_ref.shape[0]
    w3 = 3 * DN_WIDTH
    x = a_ref[:, 0:w3]
    conv = cw_ref[DN_CONV - 1:DN_CONV, :] * x
    for i in range(DN_CONV - 1):
        conv = conv + cw_ref[i:i + 1, :] * conv_ref[i]
        if i > 0:
            convn_ref[i - 1] = conv_ref[i]
    convn_ref[DN_CONV - 2] = x
    r = _silu(conv)
    bd1 = bd1_ref[...]
    q = r[:, 0:DN_WIDTH]
    k = r[:, DN_WIDTH:2 * DN_WIDTH]
    v = r[:, 2 * DN_WIDTH:w3]
    q = q * lax.rsqrt(_dot_r01(q * q, bd1, 2) + EPS) * (DN_HEAD_DIM ** -0.5)
    k = k * lax.rsqrt(_dot_r01(k * k, bd1, 2) + EPS)
    abv = ab_ref[...]
    lane8 = lax.broadcasted_iota(jnp.int32, abv.shape, 1)
    g_all = -jnp.exp(alog_ref[...]) * _softplus(abv + dtb_ref[...])
    gbx = _dot_r01(jnp.where(lane8 < DN_HEADS, g_all, _sigmoid(abv)), e2_ref[...], 3)
    eg = jnp.exp(gbx[:, 0:DN_PACK])
    bexp = gbx[:, DN_PACK:]
    qk = _dot_r01(q * k, bd1, 2)
    wk = bexp * eg * k
    qd = q * eg
    row = lax.broadcasted_iota(jnp.int32, (nseq, DN_PACK), 0)
    bdmask = (lax.broadcasted_iota(jnp.int32, (DN_PACK, DN_PACK), 0) // DN_HEAD_DIM
              == lax.broadcasted_iota(jnp.int32, (DN_PACK, DN_PACK), 1) // DN_HEAD_DIM)

    def only(s, x):
        return jnp.where(row == s, x, 0.0).astype(BF16)

    s16 = [s_ref[s].astype(BF16) for s in range(nseq)]
    ws = sum(_dot(only(s, wk), s16[s]) for s in range(nseq))
    qs = sum(_dot(only(s, qd), s16[s]) for s in range(nseq))
    vnew = bexp * v - ws
    o = qs + qk * vnew
    ms = _dot_r01(o * o, bd1, 2) * (1.0 / DN_HEAD_DIM)
    o = o * lax.rsqrt(ms + EPS) * dnn_ref[...] * _silu(a_ref[:, w3:w3 + DN_WIDTH])
    o_ref[...] = o.astype(BF16)
    v16 = vnew.astype(BF16)
    for s in range(nseq):
        outer = _dot_tn(only(s, k), v16)
        sout_ref[s] = eg[s:s + 1, :] * s_ref[s] + jnp.where(bdmask, outer, 0.0)


def _delta_step(a, ab, cw, alog, dtb, dnn, e2, bd1, conv_t, s0):
    nseq = a.shape[0]
    blk = SUBLANES
    full = lambda x: pl.BlockSpec(x.shape, lambda i: (0,) * x.ndim)
    conv_spec = pl.BlockSpec((DN_CONV - 1, blk, 3 * DN_WIDTH), lambda i: (0, i, 0))
    s_spec = pl.BlockSpec((blk, DN_PACK, DN_PACK), lambda i: (i, 0, 0))
    return pl.pallas_call(
        _delta_step_kernel,
        grid=(nseq // blk,),
        in_specs=[pl.BlockSpec((blk, A_COLS), lambda i: (i, 0)), pl.BlockSpec((blk, LANES), lambda i: (i, 0)),
                  full(cw), full(alog), full(dtb), full(dnn), full(e2), full(bd1), conv_spec, s_spec],
        out_specs=(pl.BlockSpec((blk, DN_WIDTH), lambda i: (i, 0)), conv_spec, s_spec),
        out_shape=(jax.ShapeDtypeStruct((nseq, DN_WIDTH), BF16),
                   jax.ShapeDtypeStruct(conv_t.shape, F32),
                   jax.ShapeDtypeStruct(s0.shape, F32)),
        compiler_params=_cparams(("parallel",)),
        name="gated_delta_step",
    )(a, ab, cw, alog, dtb, dnn, e2, bd1, conv_t, s0)


def _delta_constants():
    c = DN_CHUNK
    lane = jnp.arange(2 * DN_PACK)
    row = jnp.arange(LANES)[:, None]
    e2 = (((lane[None, :] < DN_PACK) & (row == lane[None, :] // c))
          | ((lane[None, :] >= DN_PACK) & (row == DN_HEADS + (lane[None, :] - DN_PACK) // c))).astype(BF16)
    i = jnp.arange(c)[:, None]
    m = jnp.arange(c)[None, :]
    lmat = jnp.concatenate([(m <= i), (m > i)], axis=0).astype(BF16)
    hh = jnp.arange(DN_PACK) // c
    bd1 = (hh[:, None] == hh[None, :]).astype(BF16)
    return e2, lmat, bd1


def _tri_incl(n):
    return (jnp.arange(n)[:, None] >= jnp.arange(n)[None, :]).astype(BF16)


def _seg_matrices():
    c = jnp.arange(2 * SB_WIDTH)[:, None] // SB_HEAD_DIM
    j = jnp.arange(LANES)[None, :]
    seg = (c == j).astype(BF16)
    return seg, seg.T


def _prep_w_in(w):
    n_ab = 2 * DN_HEADS
    c1 = A_COLS
    c2 = c1 + n_ab
    c3 = c2 + B_COLS
    pad = jnp.zeros((D_MODEL, LANES - n_ab), w.dtype)
    return jnp.concatenate([w[:, :c1], w[:, c2:c3], w[:, c3:], w[:, c1:c2], pad], axis=1).astype(BF16)


def _row(v, width):
    return jnp.pad(v.astype(F32), (0, width - v.shape[0]))[None]


def _pack_state(s):
    eye = jnp.eye(DN_HEADS, dtype=s.dtype)
    return jnp.einsum('bhkv,hg->bhkgv', s, eye).reshape(s.shape[0], DN_PACK, DN_PACK)


def _unpack_state(sp):
    b = sp.shape[0]
    sp = sp.reshape(b, DN_HEADS, DN_HEAD_DIM, DN_HEADS, DN_HEAD_DIM)
    return jnp.stack([sp[:, h, :, h, :] for h in range(DN_HEADS)], axis=1)


TM_PROJ = 256
TM_FFN = 512
TH_FFN = FFN_HIDDEN // 2
TB_DELTA = 4 * DN_CHUNK
TQ_ATTN = 256
S5_ROWS = 1024


def _mixers(x, lw, consts, bsz, t, conv0, s0, x0_ssm, attn_fn):
    (w16, qkn, cw, alog, dtb, dnn, wb, lam, cm, dskip, gw16, gb, wo16, ffn_g, wg16, wu16, wd16, attn_g) = lw
    seg, segt, e2, lmat, bd1 = consts
    n = bsz * t
    tm = min(TM_PROJ, n)
    a, ab, q16, k16, v16, kf, vf, u = _proj_in(x, attn_g, w16, qkn, seg, segt, tm)

    hist = CONV_PAD - (DN_CONV - 1)
    if t == 1 and bsz % SUBLANES == 0:
        o_a, conv_t, s_new = _delta_step(a, ab, cw, alog, dtb, dnn, e2, bd1,
                                         conv0[:, hist:].transpose(1, 0, 2), s0)
        convn = jnp.pad(conv_t.transpose(1, 0, 2), ((0, 0), (hist, 0), (0, 0)))
    else:
        tb = min(TB_DELTA, -(-t // DN_CHUNK) * DN_CHUNK)
        t_pad = -(-t // tb) * tb
        if t_pad != t:
            padt = lambda y: jnp.pad(y.reshape(bsz, t, -1), ((0, 0), (0, t_pad - t), (0, 0))).reshape(bsz * t_pad, -1)
            a_in, ab_in = padt(a), padt(ab)
        else:
            a_in, ab_in = a, ab
        o_a, convn, s_new = _delta(a_in, ab_in, cw, alog, dtb, dnn, e2, lmat, bd1, conv0, s0, bsz, t_pad, t, tb)
        if t_pad != t:
            o_a = o_a.reshape(bsz, t_pad, DN_WIDTH)[:, :t].reshape(n, DN_WIDTH)

    o_b = attn_fn(q16, k16, v16)

    nb = bsz if (bsz * 2 == SUBLANES and t % 2 == 0) else -(-bsz // SUBLANES) * SUBLANES
    steps = min(S5_ROWS // nb, t)
    u_rows = jnp.pad(u.reshape(bsz, t, SSM_WIDTH).transpose(1, 0, 2), ((0, 0), (0, nb - bsz), (0, 0)))
    o_c, xt = _s5(u_rows.reshape(t * nb, SSM_WIDTH), wb, lam, cm, dskip, gw16, gb,
                  jnp.pad(x0_ssm, ((0, nb - bsz), (0, 0))), steps, nb)
    o_c = o_c.reshape(t, nb, SSM_WIDTH)[:, :bsz].transpose(1, 0, 2).reshape(n, SSM_WIDTH)

    x = _out_ffn(x, o_a, o_b, o_c, wo16, ffn_g, wg16, wu16, wd16, min(TM_FFN, n), TH_FFN)
    leaves = (kf.reshape(bsz, t, SB_HEADS, SB_HEAD_DIM), vf.reshape(bsz, t, SB_HEADS, SB_HEAD_DIM),
              convn[:, CONV_PAD - (DN_CONV - 1):], _unpack_state(s_new),
              xt[:bsz, :SSM_FLAT].reshape(bsz, SSM_GROUPS, SSM_STATE),
              xt[:bsz, SSM_FLAT:].reshape(bsz, SSM_GROUPS, SSM_STATE))
    return x, leaves


def kernel(x_prompt, x_sample, cache_k, cache_v, page_table, state_conv, state_delta, state_ssm_re, state_ssm_im,
           attn_norm, w_in, conv_w, dn_a_log, dn_dt_bias, dn_out_norm, sb_q_norm, sb_k_norm, sb_bias,
           ssm_lambda_re, ssm_lambda_im, ssm_log_step, ssm_b_re, ssm_b_im, ssm_c_re, ssm_c_im, ssm_d,
           ssm_glu_w, ssm_glu_b, w_out, ffn_norm, w_gate, w_up, w_down):
    bp, tp, _ = x_prompt.shape
    bs, ts, _ = x_sample.shape
    assert ts == 1, "the sample group decodes one token per sequence"
    n_pool = cache_k.shape[1]
    ck = cache_k.transpose(0, 1, 3, 4, 2).reshape(DEPTH, n_pool, SB_WIDTH, PAGE_SIZE)
    cv = cache_v.transpose(0, 1, 3, 4, 2).reshape(DEPTH, n_pool, SB_WIDTH, PAGE_SIZE)
    consts = _seg_matrices() + _delta_constants()
    tri_p = _tri_incl(TQ_ATTN)
    tri_d = _tri_incl(PAGE_SIZE)

    xp = x_prompt.reshape(bp * tp, D_MODEL)
    xs = x_sample.reshape(bs * ts, D_MODEL)
    conv0_p = jnp.zeros((bp, CONV_PAD, 3 * DN_WIDTH), F32)
    s0_p = jnp.zeros((bp, DN_PACK, DN_PACK), F32)
    x0_p = jnp.zeros((bp, 2 * SSM_FLAT), F32)
    acc = [[] for _ in range(12)]
    for l in range(DEPTH):
        wb, cm, lam = _prep_s5(ssm_lambda_re[l], ssm_lambda_im[l], ssm_log_step[l],
                               ssm_b_re[l], ssm_b_im[l], ssm_c_re[l], ssm_c_im[l])
        qkn = jnp.concatenate([jnp.tile(sb_q_norm[l], SB_HEADS), jnp.tile(sb_k_norm[l], SB_HEADS)])[None]
        lw = (_prep_w_in(w_in[l]), qkn, conv_w[l], _row(dn_a_log[l], LANES), _row(dn_dt_bias[l], LANES),
              jnp.tile(dn_out_norm[l], DN_HEADS)[None], wb, lam, cm, ssm_d[l][None],
              ssm_glu_w[l].astype(BF16), ssm_glu_b[l][None], w_out[l].astype(BF16), ffn_norm[l][None],
              w_gate[l].astype(BF16), w_up[l].astype(BF16), w_down[l].astype(BF16), attn_norm[l][None])
        bias = sb_bias[l].astype(F32)

        attn_p = lambda q, k, v: _attn_prompt(q, k, v, bias, tri_p, bp, tp, TQ_ATTN)
        xp, st_p = _mixers(xp, lw, consts, bp, tp, conv0_p, s0_p, x0_p, attn_p)

        bias_col = jnp.broadcast_to(bias[:, None], (SB_HEADS, LANES))
        attn_s = lambda q, k, v: _attn_decode(q.reshape(bs, 1, SB_WIDTH), bias_col, tri_d, page_table,
                                              ck, cv, l).reshape(bs, SB_WIDTH)
        conv0_s = jnp.pad(state_conv[l], ((0, 0), (CONV_PAD - (DN_CONV - 1), 0), (0, 0)))
        x0_s = jnp.concatenate([state_ssm_re[l].reshape(bs, SSM_FLAT), state_ssm_im[l].reshape(bs, SSM_FLAT)], axis=1)
        xs, st_s = _mixers(xs, lw, consts, bs, ts, conv0_s, _pack_state(state_delta[l]), x0_s, attn_s)
        for a, val in zip(acc, st_p + st_s):
            a.append(val)
    st = [jnp.stack(a, axis=0) for a in acc]
    return (xp.reshape(bp, tp, D_MODEL), xs.reshape(bs, ts, D_MODEL), *st)
```

```python
import functools
import math

import jax
import jax.numpy as jnp
from jax import lax
from jax.experimental import pallas as pl
from jax.experimental.pallas import tpu as pltpu

F32 = jnp.float32
BF16 = jnp.bfloat16

D_MODEL = 1024
DEPTH = 4
PAGE_SIZE = 128
DN_HEADS = 4
DN_HEAD_DIM = 64
DN_WIDTH = DN_HEADS * DN_HEAD_DIM
DN_CONV = 4
DN_CHUNK = 64
SB_HEADS = 8
SB_HEAD_DIM = 64
SB_WIDTH = SB_HEADS * SB_HEAD_DIM
SSM_WIDTH = D_MODEL - DN_WIDTH - SB_WIDTH
SSM_GROUP = 16
SSM_GROUPS = SSM_WIDTH // SSM_GROUP
SSM_STATE = 64
SSM_FLAT = SSM_GROUPS * SSM_STATE
FFN_HIDDEN = 2816
EPS = 1e-6

LANES = 128
SUBLANES = 8
VMEM_LIMIT = 56 * 1024 * 1024

A_COLS = 4 * DN_WIDTH
B_COLS = 3 * SB_WIDTH
IN_PAD = A_COLS + B_COLS + SSM_WIDTH + LANES


def _cparams(sem):
    return pltpu.CompilerParams(dimension_semantics=sem, vmem_limit_bytes=VMEM_LIMIT)


def _split(x, n):
    terms = []
    r = x
    for i in range(n):
        t = r.astype(BF16)
        terms.append(t)
        if i + 1 < n:
            r = r - t.astype(F32)
    return terms


def _dot(a, b):
    return jnp.dot(a, b, preferred_element_type=F32)


def _dot_l01(a01, x, n):
    return sum(_dot(a01, t) for t in _split(x, n))


def _dot_r01(x, b01, n):
    return sum(_dot(t, b01) for t in _split(x, n))


def _sigmoid(x):
    return 1.0 / (1.0 + jnp.exp(-x))


def _silu(x):
    return x * _sigmoid(x)


LOG2E = 1.4426950408889634


def _softplus(x):
    return jnp.maximum(x, 0.0) + jnp.log(1.0 + jnp.exp2(jnp.abs(x) * (-LOG2E)))


def _proj_in_kernel(x_ref, g_ref, w_ref, qkn_ref, seg_ref, segt_ref, *refs, feature_major):
    a_ref, ab_ref, q16_ref, k16_ref, v16_ref, kf_ref, vf_ref, u_ref = refs[-8:]
    x = x_ref[...]
    ms = jnp.mean(x * x, axis=-1, keepdims=True)
    h = (x * lax.rsqrt(ms + EPS) * g_ref[...]).astype(BF16)
    a_ref[...] = _dot(h, w_ref[:, 0:A_COLS])
    c0 = A_COLS
    qk = _dot(h, w_ref[:, c0:c0 + 2 * SB_WIDTH])
    v = _dot(h, w_ref[:, c0 + 2 * SB_WIDTH:c0 + B_COLS])
    c0 += B_COLS
    u_ref[...] = _dot(h, w_ref[:, c0:c0 + SSM_WIDTH])
    c0 += SSM_WIDTH
    ab_ref[...] = _dot(h, w_ref[:, c0:c0 + LANES])
    ssq = _dot_r01(qk * qk, seg_ref[...], 2)
    inv = lax.rsqrt(ssq * (1.0 / SB_HEAD_DIM) + EPS)
    qkn = qk * _dot_r01(inv, segt_ref[...], 3) * qkn_ref[...]
    qn = qkn[:, 0:SB_WIDTH]
    kn = qkn[:, SB_WIDTH:]
    q16_ref[...] = (qn * (SB_HEAD_DIM ** -0.5)).astype(BF16)
    k16_ref[...] = kn.astype(BF16)
    v16_ref[...] = v.astype(BF16)
    if feature_major:
        kf_ref[...] = kn.T
        vf_ref[...] = v.T
    else:
        kf_ref[...] = kn
        vf_ref[...] = v


def _proj_in(x, gain, w16, qkn, seg, segt, tm, bsz, t, layer, kv_bufs):
    n = x.shape[0]
    row = lambda w: pl.BlockSpec((tm, w), lambda i: (i, 0))
    full = lambda a: pl.BlockSpec(a.shape, lambda i: (0,) * a.ndim)
    feature_major = t % tm == 0 and tm % LANES == 0
    if feature_major:
        nt = t // tm
        slab = pl.BlockSpec((None, None, SB_WIDTH, tm), lambda i: (layer, i // nt, 0, i % nt))
        kv_shape = jax.ShapeDtypeStruct((DEPTH, bsz, SB_WIDTH, t), F32)
    else:
        slab = pl.BlockSpec((None, tm, SB_WIDTH), lambda i: (layer, i, 0))
        kv_shape = jax.ShapeDtypeStruct((DEPTH, n, SB_WIDTH), F32)
    out_shapes = (
        jax.ShapeDtypeStruct((n, A_COLS), F32),
        jax.ShapeDtypeStruct((n, LANES), F32),
        jax.ShapeDtypeStruct((n, SB_WIDTH), BF16),
        jax.ShapeDtypeStruct((n, SB_WIDTH), BF16),
        jax.ShapeDtypeStruct((n, SB_WIDTH), BF16),
        kv_shape,
        kv_shape,
        jax.ShapeDtypeStruct((n, SSM_WIDTH), F32),
    )
    in_specs = [row(D_MODEL), full(gain), full(w16), full(qkn), full(seg), full(segt)]
    args = [x, gain, w16, qkn, seg, segt]
    aliases = {}
    if kv_bufs is not None:
        in_specs += [pl.BlockSpec(memory_space=pl.ANY)] * 2
        args += list(kv_bufs)
        aliases = {len(args) - 2: 5, len(args) - 1: 6}
    return pl.pallas_call(
        functools.partial(_proj_in_kernel, feature_major=feature_major),
        grid=(n // tm,),
        in_specs=in_specs,
        out_specs=(row(A_COLS), row(LANES), row(SB_WIDTH), row(SB_WIDTH), row(SB_WIDTH),
                   slab, slab, row(SSM_WIDTH)),
        out_shape=out_shapes,
        input_output_aliases=aliases,
        compiler_params=_cparams(("parallel",)),
        name="proj_in",
    )(*args)


MXU_TILE = 256


def _hidden_splits(th_max):
    assert FFN_HIDDEN % MXU_TILE == 0
    tiles = FFN_HIDDEN // MXU_TILE
    per = max(th_max // MXU_TILE, 1)
    cuts = list(range(0, tiles, per)) + [tiles]
    return [(lo * MXU_TILE, hi * MXU_TILE) for lo, hi in zip(cuts[:-1], cuts[1:])]


def _out_ffn_kernel(x_ref, oa_ref, ob_ref, oc_ref, wo_ref, g_ref, wg_ref, wu_ref, wd_ref, y_ref, *, th):
    mix = (_dot(oa_ref[...], wo_ref[0:DN_WIDTH, :])
           + _dot(ob_ref[...], wo_ref[DN_WIDTH:DN_WIDTH + SB_WIDTH, :])
           + _dot(oc_ref[...], wo_ref[DN_WIDTH + SB_WIDTH:, :]))
    x1 = x_ref[...] + mix
    ms = jnp.mean(x1 * x1, axis=-1, keepdims=True)
    h = (x1 * lax.rsqrt(ms + EPS) * g_ref[...]).astype(BF16)
    y = x1
    for lo, hi in _hidden_splits(th):
        gate = _dot(h, wg_ref[:, lo:hi])
        up = _dot(h, wu_ref[:, lo:hi])
        y = y + _dot((_silu(gate) * up).astype(BF16), wd_ref[lo:hi, :])
    y_ref[...] = y


def _out_ffn(x, oa, ob, oc, wo16, gain, wg16, wu16, wd16, tm, th):
    n = x.shape[0]
    row = lambda w: pl.BlockSpec((tm, w), lambda i: (i, 0))
    fixed = lambda a: pl.BlockSpec(a.shape, lambda i: (0,) * a.ndim, pipeline_mode=pl.Buffered(1))
    return pl.pallas_call(
        functools.partial(_out_ffn_kernel, th=th),
        grid=(n // tm,),
        in_specs=[row(D_MODEL), row(DN_WIDTH), row(SB_WIDTH), row(SSM_WIDTH),
                  fixed(wo16), fixed(gain), fixed(wg16), fixed(wu16), fixed(wd16)],
        out_specs=row(D_MODEL),
        out_shape=jax.ShapeDtypeStruct((n, D_MODEL), F32),
        compiler_params=_cparams(("parallel",)),
        name="out_ffn",
    )(x, oa, ob, oc, wo16, gain, wg16, wu16, wd16)


def _gelu_tanh(x):
    c = math.sqrt(2.0 / math.pi)
    return 0.5 * x * (1.0 + jnp.tanh(c * (x + 0.044715 * (x * x * x))))


def _s5_kernel(u_ref, wb_ref, lam_ref, cm_ref, d_ref, gw_ref, gb_ref, x0_ref,
               o_ref, xt_ref, bu_scr, x_scr, *, steps, nb):
    two_step = nb * 2 == SUBLANES
    tile = SUBLANES if two_step else nb

    @pl.when(pl.program_id(0) == 0)
    def _():
        x_scr[...] = jnp.concatenate([x0_ref[...]] * 2, axis=0) if two_step else x0_ref[...]

    u = u_ref[...]
    bu_scr[...] = _dot(u.astype(BF16), wb_ref[...])
    lre = jnp.broadcast_to(lam_ref[:, 0:SSM_FLAT], (tile, SSM_FLAT))
    lim = jnp.broadcast_to(lam_ref[:, SSM_FLAT:], (tile, SSM_FLAT))
    lower = lax.broadcasted_iota(jnp.int32, (tile, 2 * SSM_FLAT), 0) < nb

    def advance(x, bu):
        xr = x[:, 0:SSM_FLAT]
        xi = x[:, SSM_FLAT:]
        nr = lre * xr - lim * xi + bu[:, 0:SSM_FLAT]
        ni = lre * xi + lim * xr + bu[:, SSM_FLAT:]
        return jnp.concatenate([nr, ni], axis=1)

    def step(t, x):
        r0 = pl.multiple_of(t * tile, tile)
        bu = bu_scr[pl.ds(r0, tile), :]
        x1 = advance(x, bu)
        if two_step:
            x2 = advance(pltpu.roll(x1, nb, 0), bu)
            bu_scr[pl.ds(r0, tile), :] = jnp.where(lower, x1, x2)
            return jnp.where(lower, pltpu.roll(x2, nb, 0), x2)
        bu_scr[pl.ds(r0, tile), :] = x1
        return x1

    x = lax.fori_loop(0, steps * nb // tile, step, x_scr[...])
    x_scr[...] = x
    xt_ref[...] = x[0:nb, :]
    y = _dot(bu_scr[...].astype(BF16), cm_ref[...]) + d_ref[...] * u
    z = _gelu_tanh(y)
    o_ref[...] = (z * _sigmoid(_dot(z.astype(BF16), gw_ref[...]) + gb_ref[...])).astype(BF16)


def _s5(u_rows, wb, lam, cm, d, gw16, gb, x0, steps, nb):
    assert nb % SUBLANES == 0 or (nb * 2 == SUBLANES and steps % 2 == 0)
    n = u_rows.shape[0]
    rows = steps * nb
    full = lambda a: pl.BlockSpec(a.shape, lambda i: (0,) * a.ndim)
    return pl.pallas_call(
        functools.partial(_s5_kernel, steps=steps, nb=nb),
        grid=(n // rows,),
        in_specs=[pl.BlockSpec((rows, SSM_WIDTH), lambda i: (i, 0)),
                  full(wb), full(lam), full(cm), full(d), full(gw16), full(gb), full(x0)],
        out_specs=(pl.BlockSpec((rows, SSM_WIDTH), lambda i: (i, 0)), full(x0)),
        out_shape=(jax.ShapeDtypeStruct((n, SSM_WIDTH), BF16),
                   jax.ShapeDtypeStruct(x0.shape, F32)),
        scratch_shapes=[pltpu.VMEM((rows, 2 * SSM_FLAT), F32),
                        pltpu.VMEM((max(nb, SUBLANES), 2 * SSM_FLAT), F32)],
        compiler_params=_cparams(("arbitrary",)),
        name="s5_scan",
    )(u_rows, wb, lam, cm, d, gw16, gb, x0)


def _prep_s5(lam_re, lam_im, log_step, b_re, b_im, c_re, c_im):
    step = jnp.exp(log_step)[:, None]
    mag = jnp.exp(lam_re * step)
    ang = lam_im * step
    lb_re, lb_im = mag * jnp.cos(ang), mag * jnp.sin(ang)
    den = lam_re * lam_re + lam_im * lam_im
    f_re = ((lb_re - 1.0) * lam_re + lb_im * lam_im) / den
    f_im = (lb_im * lam_re - (lb_re - 1.0) * lam_im) / den
    bb_re = f_re[..., None] * b_re - f_im[..., None] * b_im
    bb_im = f_re[..., None] * b_im + f_im[..., None] * b_re
    eye = jnp.eye(SSM_GROUPS, dtype=F32)
    pack_b = lambda bb: jnp.einsum('gpc,gh->gchp', bb, eye).reshape(SSM_WIDTH, SSM_FLAT)
    pack_c = lambda cc: jnp.einsum('gcp,gh->gphc', cc, eye).reshape(SSM_FLAT, SSM_WIDTH)
    wb = jnp.concatenate([pack_b(bb_re), pack_b(bb_im)], axis=1).astype(BF16)
    cm = jnp.concatenate([pack_c(c_re), -pack_c(c_im)], axis=0).astype(BF16)
    lam = jnp.concatenate([lb_re.reshape(1, SSM_FLAT), lb_im.reshape(1, SSM_FLAT)], axis=1)
    return wb, cm, lam


HEAD_GROUP = 4
GROUP_W = HEAD_GROUP * SB_HEAD_DIM


def _dot_nt(a, b):
    return lax.dot_general(a, b, (((1,), (1,)), ((), ())), preferred_element_type=F32)


def _attn_kernel(bias_ref, q_ref, k_ref, v_ref, tri_ref, o_ref, acc_scr, z_scr, w_scr, vs_scr, *, tq):
    g = pl.program_id(1)
    i = pl.program_id(2)
    q = q_ref[...]
    lane_head = lax.broadcasted_iota(jnp.int32, (tq, GROUP_W), 1) // SB_HEAD_DIM

    @pl.when(i == 0)
    def _():
        def fill(j, _):
            vb = v_ref[pl.ds(pl.multiple_of(j * tq, tq), tq), :]
            for hh in range(HEAD_GROUP):
                vs_scr[j, hh * tq:(hh + 1) * tq, :] = jnp.where(lane_head == hh, vb, jnp.zeros_like(vb))
            return 0
        lax.fori_loop(0, vs_scr.shape[0], fill, 0)
    causal = (lax.broadcasted_iota(jnp.int32, (tq, tq), 1)
              < lax.broadcasted_iota(jnp.int32, (tq, tq), 0))
    tri = tri_ref[...]
    heads = range(HEAD_GROUP)
    qms = [jnp.where(lane_head == hh, q, jnp.zeros_like(q)) for hh in heads]
    biases = [bias_ref[g * HEAD_GROUP + hh] for hh in heads]

    r0 = pl.multiple_of(i * tq, tq)

    def suffix_sum(sp):
        return _dot(sp.astype(BF16), tri)

    def scores(c0):
        kb = k_ref[pl.ds(c0, tq), :]
        return [_dot_nt(qms[hh], kb) + biases[hh] for hh in heads]

    zs = scores(r0)
    cs = [suffix_sum(jnp.where(causal, _softplus(zs[hh]), 0.0)) for hh in heads]
    for hh in heads:
        w_scr[:, hh * tq:(hh + 1) * tq] = jnp.where(causal, jnp.exp(zs[hh] - cs[hh]), 0.0).astype(BF16)
    acc_scr[...] = jnp.zeros((tq, GROUP_W), F32)
    zn = scores(pl.multiple_of(jnp.maximum(i - 1, 0) * tq, tq))
    for hh in heads:
        z_scr[hh] = zn[hh]
    carries = tuple(cb[:, 0:1] for cb in cs)

    def body(jj, carries):
        zc = [z_scr[hh] for hh in heads]
        zn = scores(pl.multiple_of(jnp.maximum(i - jj - 1, 0) * tq, tq))
        acc_scr[...] += _dot(w_scr[...], vs_scr[i - jj + 1])
        cs = [suffix_sum(_softplus(zc[hh])) + carries[hh] for hh in heads]
        for hh in heads:
            w_scr[:, hh * tq:(hh + 1) * tq] = jnp.exp(zc[hh] - cs[hh]).astype(BF16)
            z_scr[hh] = zn[hh]
        return tuple(cb[:, 0:1] for cb in cs)

    lax.fori_loop(1, i + 1, body, carries)
    o_ref[...] = (acc_scr[...] + _dot(w_scr[...], vs_scr[0])).astype(BF16)


def _attn_prompt(q16, k16, v16, bias, tri, bsz, t, tq):
    nq = t // tq
    ngrp = SB_WIDTH // GROUP_W
    grid_spec = pltpu.PrefetchScalarGridSpec(
        num_scalar_prefetch=1,
        grid=(bsz, ngrp, nq),
        in_specs=[pl.BlockSpec((tq, GROUP_W), lambda b, g, i, s: (b * nq + i, g)),
                  pl.BlockSpec((t, GROUP_W), lambda b, g, i, s: (b, g)),
                  pl.BlockSpec((t, GROUP_W), lambda b, g, i, s: (b, g)),
                  pl.BlockSpec((tq, tq), lambda b, g, i, s: (0, 0))],
        out_specs=pl.BlockSpec((tq, GROUP_W), lambda b, g, i, s: (b * nq + i, g)),
        scratch_shapes=[pltpu.VMEM((tq, GROUP_W), F32),
                        pltpu.VMEM((HEAD_GROUP, tq, tq), F32),
                        pltpu.VMEM((tq, HEAD_GROUP * tq), BF16),
                        pltpu.VMEM((nq, HEAD_GROUP * tq, GROUP_W), BF16)],
    )
    return pl.pallas_call(
        functools.partial(_attn_kernel, tq=tq),
        grid_spec=grid_spec,
        out_shape=jax.ShapeDtypeStruct((bsz * t, SB_WIDTH), BF16),
        compiler_params=_cparams(("arbitrary", "arbitrary", "arbitrary")),
        name="sb_attn_prompt",
    )(bias, q16, k16, v16, tri)


PAGES_PER_STEP = 16


def _attn_decode_kernel(pt_ref, q_ref, bias_ref, tri_ref, *refs, pp):
    del pt_ref
    k_refs = refs[:pp]
    v_refs = refs[pp:2 * pp]
    o_ref, acc_scr, carry_scr = refs[2 * pp:]
    j = pl.program_id(1)

    @pl.when(j == 0)
    def _():
        acc_scr[...] = jnp.zeros_like(acc_scr)
        carry_scr[...] = jnp.zeros_like(carry_scr)

    on_diag = (lax.broadcasted_iota(jnp.int32, (SB_HEADS, SB_WIDTH), 0)
               == lax.broadcasted_iota(jnp.int32, (SB_HEADS, SB_WIDTH), 1) // SB_HEAD_DIM)
    q = jnp.broadcast_to(q_ref[0].astype(F32), (SB_HEADS, SB_WIDTH))
    qbd = jnp.where(on_diag, q, 0.0).astype(BF16)
    bias = bias_ref[...]
    tri = tri_ref[...]
    kcat = jnp.concatenate([k_refs[p][...].astype(BF16) for p in range(pp)], axis=1)
    zw = _dot(qbd, kcat)
    z = jnp.concatenate([zw[:, p * PAGE_SIZE:(p + 1) * PAGE_SIZE] + bias for p in range(pp)], axis=0)
    cum = _dot_r01(_softplus(z), tri, 2)
    carry = carry_scr[:, 0:1]
    offs = []
    for p in range(pp):
        offs.append(carry)
        carry = carry + cum[p * SB_HEADS:(p + 1) * SB_HEADS, 0:1]
    w = jnp.exp(z - cum - jnp.concatenate(offs, axis=0)).astype(BF16)
    wcat = jnp.concatenate([w[p * SB_HEADS:(p + 1) * SB_HEADS, :] for p in range(pp)], axis=1)
    vcat = jnp.concatenate([v_refs[p][...].astype(BF16) for p in range(pp)], axis=1)
    acc = acc_scr[...] + _dot_nt(wcat, vcat)
    acc_scr[...] = acc
    carry_scr[...] = jnp.broadcast_to(carry, carry_scr.shape)

    @pl.when(j == pl.num_programs(1) - 1)
    def _():
        o_ref[0] = jnp.sum(jnp.where(on_diag, acc, 0.0), axis=0, keepdims=True).astype(BF16)


def _attn_decode(q16, bias_col, tri, page_table, cache_k, cache_v, layer):
    nseq, npg = page_table.shape
    pp = PAGES_PER_STEP

    def page_spec(p):
        return pl.BlockSpec((None, None, SB_WIDTH, PAGE_SIZE),
                            lambda b, j, pt: (layer, pt[b, npg - 1 - (j * pp + p)], 0, 0))

    grid_spec = pltpu.PrefetchScalarGridSpec(
        num_scalar_prefetch=1,
        grid=(nseq, npg // pp),
        in_specs=[pl.BlockSpec((1, 1, SB_WIDTH), lambda b, j, pt: (b, 0, 0)),
                  pl.BlockSpec((SB_HEADS, LANES), lambda b, j, pt: (0, 0)),
                  pl.BlockSpec((PAGE_SIZE, PAGE_SIZE), lambda b, j, pt: (0, 0))]
                 + [page_spec(p) for p in range(pp)] * 2,
        out_specs=pl.BlockSpec((1, 1, SB_WIDTH), lambda b, j, pt: (b, 0, 0)),
        scratch_shapes=[pltpu.VMEM((SB_HEADS, SB_WIDTH), F32), pltpu.VMEM((SB_HEADS, LANES), F32)],
    )
    return pl.pallas_call(
        functools.partial(_attn_decode_kernel, pp=pp),
        grid_spec=grid_spec,
        out_shape=jax.ShapeDtypeStruct((nseq, 1, SB_WIDTH), BF16),
        compiler_params=_cparams(("parallel", "arbitrary")),
        name="sb_attn_decode",
    )(page_table, q16, bias_col, tri, *([cache_k] * pp), *([cache_v] * pp))


DN_PACK = DN_HEADS * DN_CHUNK
CONV_PAD = SUBLANES


def _dot_tn(a, b):
    return lax.dot_general(a, b, (((0,), (0,)), ((), ())), preferred_element_type=F32)


def _delta_kernel(a_ref, ab_ref, cw_ref, alog_ref, dtb_ref, dnn_ref, e2_ref, lmat_ref, bd1_ref,
                  conv0_ref, s0_ref, o_ref, convn_ref, sout_ref, xp_scr, s_scr,
                  *, tb, nv_last, mask_rows):
    c = DN_CHUNK
    w3 = 3 * DN_WIDTH
    j = pl.program_id(1)

    @pl.when(j == 0)
    def _():
        xp_scr[0:CONV_PAD, :] = conv0_ref[...]
        s_scr[...] = s0_ref[...]

    xp_scr[CONV_PAD:CONV_PAD + tb, :] = a_ref[:, 0:w3]
    off = CONV_PAD - (DN_CONV - 1)
    conv = cw_ref[0:1, :] * xp_scr[off:off + tb, :]
    for i in range(1, DN_CONV):
        conv = conv + cw_ref[i:i + 1, :] * xp_scr[off + i:off + i + tb, :]
    convn_ref[...] = xp_scr[nv_last:nv_last + CONV_PAD, :]
    xp_scr[0:CONV_PAD, :] = xp_scr[tb:tb + CONV_PAD, :]
    r = _silu(conv)

    ii = lax.broadcasted_iota(jnp.int32, (c, DN_PACK), 0)
    jj = lax.broadcasted_iota(jnp.int32, (c, DN_PACK), 1) % c
    strict = jj < ii
    incl = jj <= ii
    bdmask = (lax.broadcasted_iota(jnp.int32, (DN_PACK, DN_PACK), 0) // c
              == lax.broadcasted_iota(jnp.int32, (DN_PACK, DN_PACK), 1) // c)
    lane8 = lax.broadcasted_iota(jnp.int32, (c, LANES), 1)
    bd1 = bd1_ref[...]

    def bd(x16):
        return jnp.where(bdmask, jnp.concatenate([x16] * DN_HEADS, axis=0), jnp.zeros((), BF16))

    def pp(a2t, b2t):
        return _dot(a2t[0], b2t[0]) + _dot(a2t[0], b2t[1]) + _dot(a2t[1], b2t[0])

    def split_bd(x):
        return tuple(bd(t) for t in _split(x, 2))

    def level_mask(b):
        return (((ii // (2 * b)) == (jj // (2 * b))) & ((ii % (2 * b)) >= b) & ((jj % (2 * b)) < b))

    chunks = range(tb // c)
    eye = (ii == jj).astype(F32)
    levels = []
    b = 1
    while b < c:
        levels.append(level_mask(b))
        b *= 2
    qs, ks, vs, bexps, egams, kdecs, qkms, nmats = [], [], [], [], [], [], [], []
    for ci in chunks:
        rows = slice(ci * c, (ci + 1) * c)
        q = r[rows, 0:DN_WIDTH]
        k = r[rows, DN_WIDTH:2 * DN_WIDTH]
        v = r[rows, 2 * DN_WIDTH:w3]
        q = q * lax.rsqrt(_dot_r01(q * q, bd1, 2) + EPS) * (DN_HEAD_DIM ** -0.5)
        k = k * lax.rsqrt(_dot_r01(k * k, bd1, 2) + EPS)
        abv = ab_ref[rows, :]
        g_all = -jnp.exp(alog_ref[...]) * _softplus(abv + dtb_ref[...])
        gb = jnp.where(lane8 < DN_HEADS, g_all, _sigmoid(abv))
        if mask_rows:
            valid = lax.broadcasted_iota(jnp.int32, (c, 1), 0) < (nv_last - ci * c)
            q = jnp.where(valid, q, 0.0)
            k = jnp.where(valid, k, 0.0)
            v = jnp.where(valid, v, 0.0)
            gb = jnp.where(valid, gb, 0.0)
        gbx = _dot_r01(gb, e2_ref[...], 3)
        gexp = gbx[:, 0:DN_PACK]
        bexp = gbx[:, DN_PACK:]
        cums = _dot_l01(lmat_ref[...], gexp, 3)
        egam = jnp.exp(cums[0:c, :])
        kdec = k * jnp.exp(cums[c:, :])
        dmat = _dot_l01(lmat_ref[0:c, :], jnp.where(strict, gexp, 0.0), 3)
        decay = jnp.exp(jnp.where(incl, dmat, 0.0))
        k16 = k.astype(BF16)
        bdk = bd(k16)
        kk = _dot_nt(k16, bdk)
        qk = _dot_nt(q.astype(BF16), bdk)
        nmats.append(jnp.where(strict, bexp * kk * decay, 0.0))
        qkms.append(jnp.where(incl, qk * decay, 0.0))
        qs.append(q)
        ks.append(k)
        vs.append(v)
        bexps.append(bexp)
        egams.append(egam)
        kdecs.append(kdec)

    xs = [eye - jnp.where(levels[0], nm, 0.0) for nm in nmats]
    nsplit = [_split(nm, 2) for nm in nmats]
    zero16 = jnp.zeros((), BF16)
    for lm in levels[1:]:
        xsplit = [_split(x, 2) for x in xs]
        ts = [pp(xsplit[ci], tuple(bd(jnp.where(lm, n, zero16)) for n in nsplit[ci])) for ci in chunks]
        xs = [xs[ci] - pp(_split(ts[ci], 2), tuple(bd(t) for t in xsplit[ci])) for ci in chunks]
    xsplit = [_split(x, 2) for x in xs]
    us = [pp(xsplit[ci], split_bd(bexps[ci] * vs[ci])) for ci in chunks]
    ws = [pp(xsplit[ci], split_bd(bexps[ci] * egams[ci] * ks[ci])) for ci in chunks]
    prepped = [(us[ci], ws[ci], qkms[ci], qs[ci] * egams[ci], kdecs[ci], egams[ci][c - 1:c, :],
                a_ref[ci * c:(ci + 1) * c, w3:w3 + DN_WIDTH]) for ci in chunks]

    s = s_scr[...]
    for ci, (u, w, qkm, qdec, kdec, glrow, gate) in enumerate(prepped):
        s16 = s.astype(BF16)
        vnew = u - _dot(w.astype(BF16), s16)
        v16 = vnew.astype(BF16)
        o = _dot(qdec.astype(BF16), s16) + _dot(qkm.astype(BF16), bd(v16))
        s = glrow * s + jnp.where(bdmask, _dot_tn(kdec.astype(BF16), v16), 0.0)
        ms = _dot_r01(o * o, bd1, 2) * (1.0 / DN_HEAD_DIM)
        o = o * lax.rsqrt(ms + EPS) * dnn_ref[...] * _silu(gate)
        o_ref[ci * c:(ci + 1) * c, :] = o.astype(BF16)
    s_scr[...] = s
    sout_ref[...] = s


def _delta(a, ab, cw, alog, dtb, dnn, e2, lmat, bd1, conv0, s0, bsz, t_pad, t_valid, tb):
    nblk = t_pad // tb
    nv_last = t_valid - (nblk - 1) * tb
    full = lambda x: pl.BlockSpec(x.shape, lambda b, j: (0,) * x.ndim)
    per_b = lambda x: pl.BlockSpec((None,) + x.shape[1:], lambda b, j: (b,) + (0,) * (x.ndim - 1))
    return pl.pallas_call(
        functools.partial(_delta_kernel, tb=tb, nv_last=nv_last, mask_rows=(t_valid != t_pad)),
        grid=(bsz, nblk),
        in_specs=[pl.BlockSpec((tb, A_COLS), lambda b, j: (b * nblk + j, 0)),
                  pl.BlockSpec((tb, LANES), lambda b, j: (b * nblk + j, 0)),
                  full(cw), full(alog), full(dtb), full(dnn), full(e2), full(lmat), full(bd1),
                  per_b(conv0), per_b(s0)],
        out_specs=(pl.BlockSpec((tb, DN_WIDTH), lambda b, j: (b * nblk + j, 0)), per_b(conv0), per_b(s0)),
        out_shape=(jax.ShapeDtypeStruct((bsz * t_pad, DN_WIDTH), BF16),
                   jax.ShapeDtypeStruct(conv0.shape, F32),
                   jax.ShapeDtypeStruct(s0.shape, F32)),
        scratch_shapes=[pltpu.VMEM((CONV_PAD + tb, 3 * DN_WIDTH), F32), pltpu.VMEM((DN_PACK, DN_PACK), F32)],
        compiler_params=_cparams(("parallel", "arbitrary")),
        name="gated_delta",
    )(a, ab, cw, alog, dtb, dnn, e2, lmat, bd1, conv0, s0)


def _delta_step_kernel(a_ref, ab_ref, cw_ref, alog_ref, dtb_ref, dnn_ref, e2_ref, bd1_ref, conv_ref, s_ref,
                       o_ref, convn_ref, sout_ref):
    nseq = a_ref.shape[0]
    w3 = 3 * DN_WIDTH
    x = a_ref[:, 0:w3]
    conv = cw_ref[DN_CONV - 1:DN_CONV, :] * x
    for i in range(DN_CONV - 1):
        conv = conv + cw_ref[i:i + 1, :] * conv_ref[i]
        if i > 0:
            convn_ref[i - 1] = conv_ref[i]
    convn_ref[DN_CONV - 2] = x
    r = _silu(conv)
    bd1 = bd1_ref[...]
    q = r[:, 0:DN_WIDTH]
    k = r[:, DN_WIDTH:2 * DN_WIDTH]
    v = r[:, 2 * DN_WIDTH:w3]
    q = q * lax.rsqrt(_dot_r01(q * q, bd1, 2) + EPS) * (DN_HEAD_DIM ** -0.5)
    k = k * lax.rsqrt(_dot_r01(k * k, bd1, 2) + EPS)
    abv = ab_ref[...]
    lane8 = lax.broadcasted_iota(jnp.int32, abv.shape, 1)
    g_all = -jnp.exp(alog_ref[...]) * _softplus(abv + dtb_ref[...])
    gbx = _dot_r01(jnp.where(lane8 < DN_HEADS, g_all, _sigmoid(abv)), e2_ref[...], 3)
    eg = jnp.exp(gbx[:, 0:DN_PACK])
    bexp = gbx[:, DN_PACK:]
    qk = _dot_r01(q * k, bd1, 2)
    wk = bexp * eg * k
    qd = q * eg
    row = lax.broadcasted_iota(jnp.int32, (nseq, DN_PACK), 0)
    bdmask = (lax.broadcasted_iota(jnp.int32, (DN_PACK, DN_PACK), 0) // DN_HEAD_DIM
              == lax.broadcasted_iota(jnp.int32, (DN_PACK, DN_PACK), 1) // DN_HEAD_DIM)

    def only(s, x):
        return jnp.where(row == s, x, 0.0).astype(BF16)

    s16 = [s_ref[s].astype(BF16) for s in range(nseq)]
    ws = sum(_dot(only(s, wk), s16[s]) for s in range(nseq))
    qs = sum(_dot(only(s, qd), s16[s]) for s in range(nseq))
    vnew = bexp * v - ws
    o = qs + qk * vnew
    ms = _dot_r01(o * o, bd1, 2) * (1.0 / DN_HEAD_DIM)
    o = o * lax.rsqrt(ms + EPS) * dnn_ref[...] * _silu(a_ref[:, w3:w3 + DN_WIDTH])
    o_ref[...] = o.astype(BF16)
    v16 = vnew.astype(BF16)
    for s in range(nseq):
        outer = _dot_tn(only(s, k), v16)
        sout_ref[s] = eg[s:s + 1, :] * s_ref[s] + jnp.where(bdmask, outer, 0.0)


def _delta_step(a, ab, cw, alog, dtb, dnn, e2, bd1, conv_t, s0):
    nseq = a.shape[0]
    blk = SUBLANES
    full = lambda x: pl.BlockSpec(x.shape, lambda i: (0,) * x.ndim)
    conv_spec = pl.BlockSpec((DN_CONV - 1, blk, 3 * DN_WIDTH), lambda i: (0, i, 0))
    s_spec = pl.BlockSpec((blk, DN_PACK, DN_PACK), lambda i: (i, 0, 0))
    return pl.pallas_call(
        _delta_step_kernel,
        grid=(nseq // blk,),
        in_specs=[pl.BlockSpec((blk, A_COLS), lambda i: (i, 0)), pl.BlockSpec((blk, LANES), lambda i: (i, 0)),
                  full(cw), full(alog), full(dtb), full(dnn), full(e2), full(bd1), conv_spec, s_spec],
        out_specs=(pl.BlockSpec((blk, DN_WIDTH), lambda i: (i, 0)), conv_spec, s_spec),
        out_shape=(jax.ShapeDtypeStruct((nseq, DN_WIDTH), BF16),
                   jax.ShapeDtypeStruct(conv_t.shape, F32),
                   jax.ShapeDtypeStruct(s0.shape, F32)),
        compiler_params=_cparams(("parallel",)),
        name="gated_delta_step",
    )(a, ab, cw, alog, dtb, dnn, e2, bd1, conv_t, s0)


def _delta_constants():
    c = DN_CHUNK
    lane = jnp.arange(2 * DN_PACK)
    row = jnp.arange(LANES)[:, None]
    e2 = (((lane[None, :] < DN_PACK) & (row == lane[None, :] // c))
          | ((lane[None, :] >= DN_PACK) & (row == DN_HEADS + (lane[None, :] - DN_PACK) // c))).astype(BF16)
    i = jnp.arange(c)[:, None]
    m = jnp.arange(c)[None, :]
    lmat = jnp.concatenate([(m <= i), (m > i)], axis=0).astype(BF16)
    hh = jnp.arange(DN_PACK) // c
    bd1 = (hh[:, None] == hh[None, :]).astype(BF16)
    return e2, lmat, bd1


def _tri_incl(n):
    return (jnp.arange(n)[:, None] >= jnp.arange(n)[None, :]).astype(BF16)


def _seg_matrices():
    c = jnp.arange(2 * SB_WIDTH)[:, None] // SB_HEAD_DIM
    j = jnp.arange(LANES)[None, :]
    seg = (c == j).astype(BF16)
    return seg, seg.T


def _prep_w_in(w):
    n_ab = 2 * DN_HEADS
    c1 = A_COLS
    c2 = c1 + n_ab
    c3 = c2 + B_COLS
    pad = jnp.zeros((D_MODEL, LANES - n_ab), w.dtype)
    return jnp.concatenate([w[:, :c1], w[:, c2:c3], w[:, c3:], w[:, c1:c2], pad], axis=1).astype(BF16)


def _row(v, width):
    return jnp.pad(v.astype(F32), (0, width - v.shape[0]))[None]


def _pack_state(s):
    eye = jnp.eye(DN_HEADS, dtype=s.dtype)
    return jnp.einsum('bhkv,hg->bhkgv', s, eye).reshape(s.shape[0], DN_PACK, DN_PACK)


def _unpack_state(sp):
    b = sp.shape[0]
    sp = sp.reshape(b, DN_HEADS, DN_HEAD_DIM, DN_HEADS, DN_HEAD_DIM)
    return jnp.stack([sp[:, h, :, h, :] for h in range(DN_HEADS)], axis=1)


TM_PROJ = 256
TM_FFN = 512
TH_FFN = 6 * MXU_TILE
TB_DELTA = 4 * DN_CHUNK
TQ_ATTN = 256
S5_ROWS = 1024


def _mixers(x, lw, consts, bsz, t, conv0, s0, x0_ssm, attn_fn, layer, kv_bufs):
    (w16, qkn, cw, alog, dtb, dnn, wb, lam, cm, dskip, gw16, gb, wo16, ffn_g, wg16, wu16, wd16, attn_g) = lw
    seg, segt, e2, lmat, bd1 = consts
    n = bsz * t
    tm = min(TM_PROJ, n)
    a, ab, q16, k16, v16, kf, vf, u = _proj_in(x, attn_g, w16, qkn, seg, segt, tm, bsz, t, layer, kv_bufs)

    hist = CONV_PAD - (DN_CONV - 1)
    if t == 1 and bsz % SUBLANES == 0:
        o_a, conv_t, s_new = _delta_step(a, ab, cw, alog, dtb, dnn, e2, bd1,
                                         conv0[:, hist:].transpose(1, 0, 2), s0)
        convn = jnp.pad(conv_t.transpose(1, 0, 2), ((0, 0), (hist, 0), (0, 0)))
    else:
        tb = min(TB_DELTA, -(-t // DN_CHUNK) * DN_CHUNK)
        t_pad = -(-t // tb) * tb
        if t_pad != t:
            padt = lambda y: jnp.pad(y.reshape(bsz, t, -1), ((0, 0), (0, t_pad - t), (0, 0))).reshape(bsz * t_pad, -1)
            a_in, ab_in = padt(a), padt(ab)
        else:
            a_in, ab_in = a, ab
        o_a, convn, s_new = _delta(a_in, ab_in, cw, alog, dtb, dnn, e2, lmat, bd1, conv0, s0, bsz, t_pad, t, tb)
        if t_pad != t:
            o_a = o_a.reshape(bsz, t_pad, DN_WIDTH)[:, :t].reshape(n, DN_WIDTH)

    o_b = attn_fn(q16, k16, v16)

    nb = bsz if (bsz * 2 == SUBLANES and t % 2 == 0) else -(-bsz // SUBLANES) * SUBLANES
    steps = min(S5_ROWS // nb, t)
    u_rows = jnp.pad(u.reshape(bsz, t, SSM_WIDTH).transpose(1, 0, 2), ((0, 0), (0, nb - bsz), (0, 0)))
    o_c, xt = _s5(u_rows.reshape(t * nb, SSM_WIDTH), wb, lam, cm, dskip, gw16, gb,
                  jnp.pad(x0_ssm, ((0, nb - bsz), (0, 0))), steps, nb)
    o_c = o_c.reshape(t, nb, SSM_WIDTH)[:, :bsz].transpose(1, 0, 2).reshape(n, SSM_WIDTH)

    x = _out_ffn(x, o_a, o_b, o_c, wo16, ffn_g, wg16, wu16, wd16, min(TM_FFN, n), TH_FFN)
    leaves = (convn[:, CONV_PAD - (DN_CONV - 1):], _unpack_state(s_new),
              xt[:bsz, :SSM_FLAT].reshape(bsz, SSM_GROUPS, SSM_STATE),
              xt[:bsz, SSM_FLAT:].reshape(bsz, SSM_GROUPS, SSM_STATE))
    return x, (kf, vf), leaves


def kernel(x_prompt, x_sample, cache_k, cache_v, page_table, state_conv, state_delta, state_ssm_re, state_ssm_im,
           attn_norm, w_in, conv_w, dn_a_log, dn_dt_bias, dn_out_norm, sb_q_norm, sb_k_norm, sb_bias,
           ssm_lambda_re, ssm_lambda_im, ssm_log_step, ssm_b_re, ssm_b_im, ssm_c_re, ssm_c_im, ssm_d,
           ssm_glu_w, ssm_glu_b, w_out, ffn_norm, w_gate, w_up, w_down):
    bp, tp, _ = x_prompt.shape
    bs, ts, _ = x_sample.shape
    assert ts == 1, "the sample group decodes one token per sequence"
    n_pool = cache_k.shape[1]
    ck = cache_k.transpose(0, 1, 3, 4, 2).reshape(DEPTH, n_pool, SB_WIDTH, PAGE_SIZE)
    cv = cache_v.transpose(0, 1, 3, 4, 2).reshape(DEPTH, n_pool, SB_WIDTH, PAGE_SIZE)
    consts = _seg_matrices() + _delta_constants()
    tri_p = _tri_incl(TQ_ATTN)
    tri_d = _tri_incl(PAGE_SIZE)

    xp = x_prompt.reshape(bp * tp, D_MODEL)
    xs = x_sample.reshape(bs * ts, D_MODEL)
    conv0_p = jnp.zeros((bp, CONV_PAD, 3 * DN_WIDTH), F32)
    s0_p = jnp.zeros((bp, DN_PACK, DN_PACK), F32)
    x0_p = jnp.zeros((bp, 2 * SSM_FLAT), F32)
    acc = [[] for _ in range(8)]
    kv_p = kv_s = None
    for l in range(DEPTH):
        wb, cm, lam = _prep_s5(ssm_lambda_re[l], ssm_lambda_im[l], ssm_log_step[l],
                               ssm_b_re[l], ssm_b_im[l], ssm_c_re[l], ssm_c_im[l])
        qkn = jnp.concatenate([jnp.tile(sb_q_norm[l], SB_HEADS), jnp.tile(sb_k_norm[l], SB_HEADS)])[None]
        lw = (_prep_w_in(w_in[l]), qkn, conv_w[l], _row(dn_a_log[l], LANES), _row(dn_dt_bias[l], LANES),
              jnp.tile(dn_out_norm[l], DN_HEADS)[None], wb, lam, cm, ssm_d[l][None],
              ssm_glu_w[l].astype(BF16), ssm_glu_b[l][None], w_out[l].astype(BF16), ffn_norm[l][None],
              w_gate[l].astype(BF16), w_up[l].astype(BF16), w_down[l].astype(BF16), attn_norm[l][None])
        bias = sb_bias[l].astype(F32)

        attn_p = lambda q, k, v: _attn_prompt(q, k, v, bias, tri_p, bp, tp, TQ_ATTN)
        xp, kv_p, st_p = _mixers(xp, lw, consts, bp, tp, conv0_p, s0_p, x0_p, attn_p, l, kv_p)

        bias_col = jnp.broadcast_to(bias[:, None], (SB_HEADS, LANES))
        attn_s = lambda q, k, v: _attn_decode(q.reshape(bs, 1, SB_WIDTH), bias_col, tri_d, page_table,
                                              ck, cv, l).reshape(bs, SB_WIDTH)
        conv0_s = jnp.pad(state_conv[l], ((0, 0), (CONV_PAD - (DN_CONV - 1), 0), (0, 0)))
        x0_s = jnp.concatenate([state_ssm_re[l].reshape(bs, SSM_FLAT), state_ssm_im[l].reshape(bs, SSM_FLAT)], axis=1)
        xs, kv_s, st_s = _mixers(xs, lw, consts, bs, ts, conv0_s, _pack_state(state_delta[l]), x0_s, attn_s, l, kv_s)
        for a, val in zip(acc, st_p + st_s):
            a.append(val)
    st = [jnp.stack(a, axis=0) for a in acc]
    def heads(buf, b, t):
        if buf.ndim == 4:
            return buf.reshape(DEPTH, b, SB_HEADS, SB_HEAD_DIM, t).transpose(0, 1, 4, 2, 3)
        return buf.reshape(DEPTH, b, t, SB_HEADS, SB_HEAD_DIM)

    return (xp.reshape(bp, tp, D_MODEL), xs.reshape(bs, ts, D_MODEL),
            heads(kv_p[0], bp, tp), heads(kv_p[1], bp, tp), *st[:4],
            heads(kv_s[0], bs, ts), heads(kv_s[1], bs, ts), *st[4:])
```

```python
import functools
import math

import jax
import jax.numpy as jnp
from jax import lax
from jax.experimental import pallas as pl
from jax.experimental.pallas import tpu as pltpu

F32 = jnp.float32
BF16 = jnp.bfloat16

D_MODEL = 1024
DEPTH = 4
PAGE_SIZE = 128
DN_HEADS = 4
DN_HEAD_DIM = 64
DN_WIDTH = DN_HEADS * DN_HEAD_DIM
DN_CONV = 4
DN_CHUNK = 64
SB_HEADS = 8
SB_HEAD_DIM = 64
SB_WIDTH = SB_HEADS * SB_HEAD_DIM
SSM_WIDTH = D_MODEL - DN_WIDTH - SB_WIDTH
SSM_GROUP = 16
SSM_GROUPS = SSM_WIDTH // SSM_GROUP
SSM_STATE = 64
SSM_FLAT = SSM_GROUPS * SSM_STATE
FFN_HIDDEN = 2816
EPS = 1e-6

LANES = 128
SUBLANES = 8
VMEM_LIMIT = 56 * 1024 * 1024

A_COLS = 4 * DN_WIDTH
B_COLS = 3 * SB_WIDTH
IN_PAD = A_COLS + B_COLS + SSM_WIDTH + LANES


def _cparams(sem):
    return pltpu.CompilerParams(dimension_semantics=sem, vmem_limit_bytes=VMEM_LIMIT)


def _split(x, n):
    terms = []
    r = x
    for i in range(n):
        t = r.astype(BF16)
        terms.append(t)
        if i + 1 < n:
            r = r - t.astype(F32)
    return terms


def _dot(a, b):
    return jnp.dot(a, b, preferred_element_type=F32)


def _dot_l01(a01, x, n):
    return sum(_dot(a01, t) for t in _split(x, n))


def _dot_r01(x, b01, n):
    return sum(_dot(t, b01) for t in _split(x, n))


def _sigmoid(x):
    return 1.0 / (1.0 + jnp.exp(-x))


def _silu(x):
    return x * _sigmoid(x)


LOG2E = 1.4426950408889634


def _softplus(x):
    return jnp.maximum(x, 0.0) + jnp.log(1.0 + jnp.exp2(jnp.abs(x) * (-LOG2E)))


def _proj_in_kernel(x_ref, g_ref, w_ref, qkn_ref, seg_ref, segt_ref, *refs, feature_major):
    a_ref, ab_ref, q16_ref, k16_ref, v16_ref, kf_ref, vf_ref, u_ref = refs[-8:]
    x = x_ref[...]
    ms = jnp.mean(x * x, axis=-1, keepdims=True)
    h = (x * lax.rsqrt(ms + EPS) * g_ref[...]).astype(BF16)
    a_ref[...] = _dot(h, w_ref[:, 0:A_COLS])
    c0 = A_COLS
    qk = _dot(h, w_ref[:, c0:c0 + 2 * SB_WIDTH])
    v = _dot(h, w_ref[:, c0 + 2 * SB_WIDTH:c0 + B_COLS])
    c0 += B_COLS
    u_ref[...] = _dot(h, w_ref[:, c0:c0 + SSM_WIDTH])
    c0 += SSM_WIDTH
    ab_ref[...] = _dot(h, w_ref[:, c0:c0 + LANES])
    ssq = _dot_r01(qk * qk, seg_ref[...], 1)
    inv = lax.rsqrt(ssq * (1.0 / SB_HEAD_DIM) + EPS)
    qkn = qk * _dot_r01(inv, segt_ref[...], 2) * qkn_ref[...]
    qn = qkn[:, 0:SB_WIDTH]
    kn = qkn[:, SB_WIDTH:]
    q16_ref[...] = (qn * (SB_HEAD_DIM ** -0.5)).astype(BF16)
    k16_ref[...] = kn.astype(BF16)
    v16_ref[...] = v.astype(BF16)
    if feature_major:
        kf_ref[...] = kn.T
        vf_ref[...] = v.T
    else:
        kf_ref[...] = kn
        vf_ref[...] = v


def _proj_in(x, gain, w16, qkn, seg, segt, tm, bsz, t, layer, kv_bufs):
    n = x.shape[0]
    row = lambda w: pl.BlockSpec((tm, w), lambda i: (i, 0))
    full = lambda a: pl.BlockSpec(a.shape, lambda i: (0,) * a.ndim)
    feature_major = t % tm == 0 and tm % LANES == 0
    if feature_major:
        nt = t // tm
        slab = pl.BlockSpec((None, None, SB_WIDTH, tm), lambda i: (layer, i // nt, 0, i % nt))
        kv_shape = jax.ShapeDtypeStruct((DEPTH, bsz, SB_WIDTH, t), F32)
        u_spec = pl.BlockSpec((tm, SSM_WIDTH), lambda i: (i % nt, i // nt))
        u_shape = jax.ShapeDtypeStruct((t, bsz * SSM_WIDTH), F32)
    else:
        slab = pl.BlockSpec((None, tm, SB_WIDTH), lambda i: (layer, i, 0))
        kv_shape = jax.ShapeDtypeStruct((DEPTH, n, SB_WIDTH), F32)
        u_spec = row(SSM_WIDTH)
        u_shape = jax.ShapeDtypeStruct((n, SSM_WIDTH), F32)
    out_shapes = (
        jax.ShapeDtypeStruct((n, A_COLS), F32),
        jax.ShapeDtypeStruct((n, LANES), F32),
        jax.ShapeDtypeStruct((n, SB_WIDTH), BF16),
        jax.ShapeDtypeStruct((n, SB_WIDTH), BF16),
        jax.ShapeDtypeStruct((n, SB_WIDTH), BF16),
        kv_shape,
        kv_shape,
        u_shape,
    )
    in_specs = [row(D_MODEL), full(gain), full(w16), full(qkn), full(seg), full(segt)]
    args = [x, gain, w16, qkn, seg, segt]
    aliases = {}
    if kv_bufs is not None:
        in_specs += [pl.BlockSpec(memory_space=pl.ANY)] * 2
        args += list(kv_bufs)
        aliases = {len(args) - 2: 5, len(args) - 1: 6}
    outs = pl.pallas_call(
        functools.partial(_proj_in_kernel, feature_major=feature_major),
        grid=(n // tm,),
        in_specs=in_specs,
        out_specs=(row(A_COLS), row(LANES), row(SB_WIDTH), row(SB_WIDTH), row(SB_WIDTH),
                   slab, slab, u_spec),
        out_shape=out_shapes,
        input_output_aliases=aliases,
        compiler_params=_cparams(("parallel",)),
        name="proj_in",
    )(*args)
    return outs[:7] + (outs[7].reshape(n, SSM_WIDTH), feature_major)


MXU_TILE = 256


def _hidden_splits(th_max):
    assert FFN_HIDDEN % MXU_TILE == 0
    tiles = FFN_HIDDEN // MXU_TILE
    per = max(th_max // MXU_TILE, 1)
    cuts = list(range(0, tiles, per)) + [tiles]
    return [(lo * MXU_TILE, hi * MXU_TILE) for lo, hi in zip(cuts[:-1], cuts[1:])]


def _out_ffn_kernel(x_ref, oa_ref, ob_ref, oc_ref, wo_ref, g_ref, wg_ref, wu_ref, wd_ref, y_ref, *, th):
    mix = (_dot(oa_ref[...], wo_ref[0:DN_WIDTH, :])
           + _dot(ob_ref[...], wo_ref[DN_WIDTH:DN_WIDTH + SB_WIDTH, :])
           + _dot(oc_ref[...], wo_ref[DN_WIDTH + SB_WIDTH:, :]))
    x1 = x_ref[...] + mix
    ms = jnp.mean(x1 * x1, axis=-1, keepdims=True)
    h = (x1 * lax.rsqrt(ms + EPS) * g_ref[...]).astype(BF16)
    y = x1
    for lo, hi in _hidden_splits(th):
        gate = _dot(h, wg_ref[:, lo:hi])
        up = _dot(h, wu_ref[:, lo:hi])
        y = y + _dot((_silu(gate) * up).astype(BF16), wd_ref[lo:hi, :])
    y_ref[...] = y


def _out_ffn(x, oa, ob, oc, wo16, gain, wg16, wu16, wd16, tm, th, oc_time_major_t=None):
    n = x.shape[0]
    row = lambda w: pl.BlockSpec((tm, w), lambda i: (i, 0))
    fixed = lambda a: pl.BlockSpec(a.shape, lambda i: (0,) * a.ndim, pipeline_mode=pl.Buffered(1))
    oc_spec = row(SSM_WIDTH)
    if oc_time_major_t is not None:
        t = oc_time_major_t
        nt = t // tm
        oc = oc.reshape(t, (n // t) * SSM_WIDTH)
        oc_spec = pl.BlockSpec((tm, SSM_WIDTH), lambda i: (i % nt, i // nt))
    return pl.pallas_call(
        functools.partial(_out_ffn_kernel, th=th),
        grid=(n // tm,),
        in_specs=[row(D_MODEL), row(DN_WIDTH), row(SB_WIDTH), oc_spec,
                  fixed(wo16), fixed(gain), fixed(wg16), fixed(wu16), fixed(wd16)],
        out_specs=row(D_MODEL),
        out_shape=jax.ShapeDtypeStruct((n, D_MODEL), F32),
        compiler_params=_cparams(("parallel",)),
        name="out_ffn",
    )(x, oa, ob, oc, wo16, gain, wg16, wu16, wd16)


def _gelu_tanh(x):
    c = math.sqrt(2.0 / math.pi)
    return 0.5 * x * (1.0 + jnp.tanh(c * (x + 0.044715 * (x * x * x))))


def _s5_kernel(u_ref, wb_ref, lam_ref, cm_ref, d_ref, gw_ref, gb_ref, x0_ref,
               o_ref, xt_ref, bu_scr, x_scr, *, steps, nb):
    two_step = nb * 2 == SUBLANES
    tile = SUBLANES if two_step else nb

    @pl.when(pl.program_id(0) == 0)
    def _():
        x_scr[...] = jnp.concatenate([x0_ref[...]] * 2, axis=0) if two_step else x0_ref[...]

    u = u_ref[...]
    bu_scr[...] = _dot(u.astype(BF16), wb_ref[...])
    lre = jnp.broadcast_to(lam_ref[:, 0:SSM_FLAT], (tile, SSM_FLAT))
    lim = jnp.broadcast_to(lam_ref[:, SSM_FLAT:], (tile, SSM_FLAT))
    lower = lax.broadcasted_iota(jnp.int32, (tile, 2 * SSM_FLAT), 0) < nb

    def advance(x, bu):
        xr = x[:, 0:SSM_FLAT]
        xi = x[:, SSM_FLAT:]
        nr = lre * xr - lim * xi + bu[:, 0:SSM_FLAT]
        ni = lre * xi + lim * xr + bu[:, SSM_FLAT:]
        return jnp.concatenate([nr, ni], axis=1)

    def step(t, x):
        r0 = pl.multiple_of(t * tile, tile)
        bu = bu_scr[pl.ds(r0, tile), :]
        x1 = advance(x, bu)
        if two_step:
            x2 = advance(pltpu.roll(x1, nb, 0), bu)
            bu_scr[pl.ds(r0, tile), :] = jnp.where(lower, x1, x2)
            return jnp.where(lower, pltpu.roll(x2, nb, 0), x2)
        bu_scr[pl.ds(r0, tile), :] = x1
        return x1

    x = lax.fori_loop(0, steps * nb // tile, step, x_scr[...])
    x_scr[...] = x
    xt_ref[...] = x[0:nb, :]
    y = _dot(bu_scr[...].astype(BF16), cm_ref[...]) + d_ref[...] * u
    z = _gelu_tanh(y)
    o_ref[...] = (z * _sigmoid(_dot(z.astype(BF16), gw_ref[...]) + gb_ref[...])).astype(BF16)


def _s5(u_rows, wb, lam, cm, d, gw16, gb, x0, steps, nb):
    assert nb % SUBLANES == 0 or (nb * 2 == SUBLANES and steps % 2 == 0)
    n = u_rows.shape[0]
    rows = steps * nb
    full = lambda a: pl.BlockSpec(a.shape, lambda i: (0,) * a.ndim)
    return pl.pallas_call(
        functools.partial(_s5_kernel, steps=steps, nb=nb),
        grid=(n // rows,),
        in_specs=[pl.BlockSpec((rows, SSM_WIDTH), lambda i: (i, 0)),
                  full(wb), full(lam), full(cm), full(d), full(gw16), full(gb), full(x0)],
        out_specs=(pl.BlockSpec((rows, SSM_WIDTH), lambda i: (i, 0)), full(x0)),
        out_shape=(jax.ShapeDtypeStruct((n, SSM_WIDTH), BF16),
                   jax.ShapeDtypeStruct(x0.shape, F32)),
        scratch_shapes=[pltpu.VMEM((rows, 2 * SSM_FLAT), F32),
                        pltpu.VMEM((max(nb, SUBLANES), 2 * SSM_FLAT), F32)],
        compiler_params=_cparams(("arbitrary",)),
        name="s5_scan",
    )(u_rows, wb, lam, cm, d, gw16, gb, x0)


def _prep_s5(lam_re, lam_im, log_step, b_re, b_im, c_re, c_im):
    step = jnp.exp(log_step)[:, None]
    mag = jnp.exp(lam_re * step)
    ang = lam_im * step
    lb_re, lb_im = mag * jnp.cos(ang), mag * jnp.sin(ang)
    den = lam_re * lam_re + lam_im * lam_im
    f_re = ((lb_re - 1.0) * lam_re + lb_im * lam_im) / den
    f_im = (lb_im * lam_re - (lb_re - 1.0) * lam_im) / den
    bb_re = f_re[..., None] * b_re - f_im[..., None] * b_im
    bb_im = f_re[..., None] * b_im + f_im[..., None] * b_re
    eye = jnp.eye(SSM_GROUPS, dtype=F32)
    pack_b = lambda bb: jnp.einsum('gpc,gh->gchp', bb, eye).reshape(SSM_WIDTH, SSM_FLAT)
    pack_c = lambda cc: jnp.einsum('gcp,gh->gphc', cc, eye).reshape(SSM_FLAT, SSM_WIDTH)
    wb = jnp.concatenate([pack_b(bb_re), pack_b(bb_im)], axis=1).astype(BF16)
    cm = jnp.concatenate([pack_c(c_re), -pack_c(c_im)], axis=0).astype(BF16)
    lam = jnp.concatenate([lb_re.reshape(1, SSM_FLAT), lb_im.reshape(1, SSM_FLAT)], axis=1)
    return wb, cm, lam


HEAD_GROUP = 4
GROUP_W = HEAD_GROUP * SB_HEAD_DIM


def _dot_nt(a, b):
    return lax.dot_general(a, b, (((1,), (1,)), ((), ())), preferred_element_type=F32)


def _attn_kernel(bias_ref, q_ref, k_ref, v_ref, tri_ref, o_ref, acc_scr, z_scr, w_scr, vs_scr, *, tq):
    g = pl.program_id(1)
    i = pl.program_id(2)
    q = q_ref[...]
    lane_head = lax.broadcasted_iota(jnp.int32, (tq, GROUP_W), 1) // SB_HEAD_DIM

    @pl.when(i == 0)
    def _():
        def fill(j, _):
            vb = v_ref[pl.ds(pl.multiple_of(j * tq, tq), tq), :]
            for hh in range(HEAD_GROUP):
                vs_scr[j, hh * tq:(hh + 1) * tq, :] = jnp.where(lane_head == hh, vb, jnp.zeros_like(vb))
            return 0
        lax.fori_loop(0, vs_scr.shape[0], fill, 0)
    causal = (lax.broadcasted_iota(jnp.int32, (tq, tq), 1)
              < lax.broadcasted_iota(jnp.int32, (tq, tq), 0))
    tri = tri_ref[...]
    heads = range(HEAD_GROUP)
    qms = [jnp.where(lane_head == hh, q, jnp.zeros_like(q)) for hh in heads]
    biases = [bias_ref[g * HEAD_GROUP + hh] for hh in heads]

    r0 = pl.multiple_of(i * tq, tq)

    def suffix_sum(sp):
        return _dot(sp.astype(BF16), tri)

    def scores(c0):
        kb = k_ref[pl.ds(c0, tq), :]
        return [_dot_nt(qms[hh], kb) + biases[hh] for hh in heads]

    zs = scores(r0)
    cs = [suffix_sum(jnp.where(causal, _softplus(zs[hh]), 0.0)) for hh in heads]
    for hh in heads:
        w_scr[:, hh * tq:(hh + 1) * tq] = jnp.where(causal, jnp.exp(zs[hh] - cs[hh]), 0.0).astype(BF16)
    acc_scr[...] = jnp.zeros((tq, GROUP_W), F32)
    zn = scores(pl.multiple_of(jnp.maximum(i - 1, 0) * tq, tq))
    for hh in heads:
        z_scr[hh] = zn[hh]
    carries = tuple(cb[:, 0:1] for cb in cs)

    def body(jj, carries):
        zc = [z_scr[hh] for hh in heads]
        zn = scores(pl.multiple_of(jnp.maximum(i - jj - 1, 0) * tq, tq))
        acc_scr[...] += _dot(w_scr[...], vs_scr[i - jj + 1])
        cs = [suffix_sum(_softplus(zc[hh])) + carries[hh] for hh in heads]
        for hh in heads:
            w_scr[:, hh * tq:(hh + 1) * tq] = jnp.exp(zc[hh] - cs[hh]).astype(BF16)
            z_scr[hh] = zn[hh]
        return tuple(cb[:, 0:1] for cb in cs)

    lax.fori_loop(1, i + 1, body, carries)
    o_ref[...] = (acc_scr[...] + _dot(w_scr[...], vs_scr[0])).astype(BF16)


def _attn_prompt(q16, k16, v16, bias, tri, bsz, t, tq):
    nq = t // tq
    ngrp = SB_WIDTH // GROUP_W
    grid_spec = pltpu.PrefetchScalarGridSpec(
        num_scalar_prefetch=1,
        grid=(bsz, ngrp, nq),
        in_specs=[pl.BlockSpec((tq, GROUP_W), lambda b, g, i, s: (b * nq + i, g)),
                  pl.BlockSpec((t, GROUP_W), lambda b, g, i, s: (b, g)),
                  pl.BlockSpec((t, GROUP_W), lambda b, g, i, s: (b, g)),
                  pl.BlockSpec((tq, tq), lambda b, g, i, s: (0, 0))],
        out_specs=pl.BlockSpec((tq, GROUP_W), lambda b, g, i, s: (b * nq + i, g)),
        scratch_shapes=[pltpu.VMEM((tq, GROUP_W), F32),
                        pltpu.VMEM((HEAD_GROUP, tq, tq), F32),
                        pltpu.VMEM((tq, HEAD_GROUP * tq), BF16),
                        pltpu.VMEM((nq, HEAD_GROUP * tq, GROUP_W), BF16)],
    )
    return pl.pallas_call(
        functools.partial(_attn_kernel, tq=tq),
        grid_spec=grid_spec,
        out_shape=jax.ShapeDtypeStruct((bsz * t, SB_WIDTH), BF16),
        compiler_params=_cparams(("arbitrary", "arbitrary", "arbitrary")),
        name="sb_attn_prompt",
    )(bias, q16, k16, v16, tri)


PAGES_PER_STEP = 16


def _attn_decode_kernel(pt_ref, q_ref, bias_ref, tri_ref, *refs, pp):
    del pt_ref
    k_refs = refs[:pp]
    v_refs = refs[pp:2 * pp]
    o_ref, acc_scr, carry_scr = refs[2 * pp:]
    j = pl.program_id(1)

    @pl.when(j == 0)
    def _():
        acc_scr[...] = jnp.zeros_like(acc_scr)
        carry_scr[...] = jnp.zeros_like(carry_scr)

    on_diag = (lax.broadcasted_iota(jnp.int32, (SB_HEADS, SB_WIDTH), 0)
               == lax.broadcasted_iota(jnp.int32, (SB_HEADS, SB_WIDTH), 1) // SB_HEAD_DIM)
    q = jnp.broadcast_to(q_ref[0].astype(F32), (SB_HEADS, SB_WIDTH))
    qbd = jnp.where(on_diag, q, 0.0).astype(BF16)
    bias = bias_ref[...]
    tri = tri_ref[...]
    kcat = jnp.concatenate([k_refs[p][...].astype(BF16) for p in range(pp)], axis=1)
    zw = _dot(qbd, kcat)
    z = jnp.concatenate([zw[:, p * PAGE_SIZE:(p + 1) * PAGE_SIZE] + bias for p in range(pp)], axis=0)
    cum = _dot_r01(_softplus(z), tri, 2)
    carry = carry_scr[:, 0:1]
    offs = []
    for p in range(pp):
        offs.append(carry)
        carry = carry + cum[p * SB_HEADS:(p + 1) * SB_HEADS, 0:1]
    w = jnp.exp(z - cum - jnp.concatenate(offs, axis=0)).astype(BF16)
    wcat = jnp.concatenate([w[p * SB_HEADS:(p + 1) * SB_HEADS, :] for p in range(pp)], axis=1)
    vcat = jnp.concatenate([v_refs[p][...].astype(BF16) for p in range(pp)], axis=1)
    acc = acc_scr[...] + _dot_nt(wcat, vcat)
    acc_scr[...] = acc
    carry_scr[...] = jnp.broadcast_to(carry, carry_scr.shape)

    @pl.when(j == pl.num_programs(1) - 1)
    def _():
        o_ref[0] = jnp.sum(jnp.where(on_diag, acc, 0.0), axis=0, keepdims=True).astype(BF16)


def _attn_decode(q16, bias_col, tri, page_table, cache_k, cache_v, layer):
    nseq, npg = page_table.shape
    pp = PAGES_PER_STEP

    def page_spec(p):
        return pl.BlockSpec((None, None, SB_WIDTH, PAGE_SIZE),
                            lambda b, j, pt: (layer, pt[b, npg - 1 - (j * pp + p)], 0, 0))

    grid_spec = pltpu.PrefetchScalarGridSpec(
        num_scalar_prefetch=1,
        grid=(nseq, npg // pp),
        in_specs=[pl.BlockSpec((1, 1, SB_WIDTH), lambda b, j, pt: (b, 0, 0)),
                  pl.BlockSpec((SB_HEADS, LANES), lambda b, j, pt: (0, 0)),
                  pl.BlockSpec((PAGE_SIZE, PAGE_SIZE), lambda b, j, pt: (0, 0))]
                 + [page_spec(p) for p in range(pp)] * 2,
        out_specs=pl.BlockSpec((1, 1, SB_WIDTH), lambda b, j, pt: (b, 0, 0)),
        scratch_shapes=[pltpu.VMEM((SB_HEADS, SB_WIDTH), F32), pltpu.VMEM((SB_HEADS, LANES), F32)],
    )
    return pl.pallas_call(
        functools.partial(_attn_decode_kernel, pp=pp),
        grid_spec=grid_spec,
        out_shape=jax.ShapeDtypeStruct((nseq, 1, SB_WIDTH), BF16),
        compiler_params=_cparams(("parallel", "arbitrary")),
        name="sb_attn_decode",
    )(page_table, q16, bias_col, tri, *([cache_k] * pp), *([cache_v] * pp))


DN_PACK = DN_HEADS * DN_CHUNK
CONV_PAD = SUBLANES


def _dot_tn(a, b):
    return lax.dot_general(a, b, (((0,), (0,)), ((), ())), preferred_element_type=F32)


def _delta_kernel(a_ref, ab_ref, cw_ref, alog_ref, dtb_ref, dnn_ref, e2_ref, lmat_ref, bd1_ref,
                  conv0_ref, s0_ref, o_ref, convn_ref, sout_ref, xp_scr, s_scr,
                  *, tb, nv_last, mask_rows):
    c = DN_CHUNK
    w3 = 3 * DN_WIDTH
    j = pl.program_id(1)

    @pl.when(j == 0)
    def _():
        xp_scr[0:CONV_PAD, :] = conv0_ref[...]
        s_scr[...] = s0_ref[...]

    xp_scr[CONV_PAD:CONV_PAD + tb, :] = a_ref[:, 0:w3]
    off = CONV_PAD - (DN_CONV - 1)
    conv = cw_ref[0:1, :] * xp_scr[off:off + tb, :]
    for i in range(1, DN_CONV):
        conv = conv + cw_ref[i:i + 1, :] * xp_scr[off + i:off + i + tb, :]
    convn_ref[...] = xp_scr[nv_last:nv_last + CONV_PAD, :]
    xp_scr[0:CONV_PAD, :] = xp_scr[tb:tb + CONV_PAD, :]
    r = _silu(conv)

    ii = lax.broadcasted_iota(jnp.int32, (c, DN_PACK), 0)
    jj = lax.broadcasted_iota(jnp.int32, (c, DN_PACK), 1) % c
    strict = jj < ii
    incl = jj <= ii
    bdmask = (lax.broadcasted_iota(jnp.int32, (DN_PACK, DN_PACK), 0) // c
              == lax.broadcasted_iota(jnp.int32, (DN_PACK, DN_PACK), 1) // c)
    lane8 = lax.broadcasted_iota(jnp.int32, (c, LANES), 1)
    bd1 = bd1_ref[...]

    def bd(x16):
        return jnp.where(bdmask, jnp.concatenate([x16] * DN_HEADS, axis=0), jnp.zeros((), BF16))

    def pp(a2t, b2t):
        return _dot(a2t[0], b2t[0]) + _dot(a2t[0], b2t[1]) + _dot(a2t[1], b2t[0])

    def split_bd(x):
        return tuple(bd(t) for t in _split(x, 2))

    def level_mask(b):
        return (((ii // (2 * b)) == (jj // (2 * b))) & ((ii % (2 * b)) >= b) & ((jj % (2 * b)) < b))

    chunks = range(tb // c)
    eye = (ii == jj).astype(F32)
    levels = []
    b = 1
    while b < c:
        levels.append(level_mask(b))
        b *= 2
    qs, ks, vs, bexps, egams, kdecs, qkms, nmats = [], [], [], [], [], [], [], []
    for ci in chunks:
        rows = slice(ci * c, (ci + 1) * c)
        q = r[rows, 0:DN_WIDTH]
        k = r[rows, DN_WIDTH:2 * DN_WIDTH]
        v = r[rows, 2 * DN_WIDTH:w3]
        q = q * lax.rsqrt(_dot_r01(q * q, bd1, 1) + EPS) * (DN_HEAD_DIM ** -0.5)
        k = k * lax.rsqrt(_dot_r01(k * k, bd1, 1) + EPS)
        abv = ab_ref[rows, :]
        g_all = -jnp.exp(alog_ref[...]) * _softplus(abv + dtb_ref[...])
        gb = jnp.where(lane8 < DN_HEADS, g_all, _sigmoid(abv))
        if mask_rows:
            valid = lax.broadcasted_iota(jnp.int32, (c, 1), 0) < (nv_last - ci * c)
            q = jnp.where(valid, q, 0.0)
            k = jnp.where(valid, k, 0.0)
            v = jnp.where(valid, v, 0.0)
            gb = jnp.where(valid, gb, 0.0)
        gbx = _dot_r01(gb, e2_ref[...], 3)
        gexp = gbx[:, 0:DN_PACK]
        bexp = gbx[:, DN_PACK:]
        cums = _dot_l01(lmat_ref[...], gexp, 3)
        egam = jnp.exp(cums[0:c, :])
        kdec = k * jnp.exp(cums[c:, :])
        dmat = _dot_l01(lmat_ref[0:c, :], jnp.where(strict, gexp, 0.0), 3)
        decay = jnp.exp(jnp.where(incl, dmat, 0.0))
        k16 = k.astype(BF16)
        bdk = bd(k16)
        kk = _dot_nt(k16, bdk)
        qk = _dot_nt(q.astype(BF16), bdk)
        nmats.append(jnp.where(strict, bexp * kk * decay, 0.0))
        qkms.append(jnp.where(incl, qk * decay, 0.0))
        qs.append(q)
        ks.append(k)
        vs.append(v)
        bexps.append(bexp)
        egams.append(egam)
        kdecs.append(kdec)

    xs = [eye - jnp.where(levels[0], nm, 0.0) for nm in nmats]
    nsplit = [_split(nm, 2) for nm in nmats]
    zero16 = jnp.zeros((), BF16)
    for lm in levels[1:]:
        xsplit = [_split(x, 2) for x in xs]
        ts = [pp(xsplit[ci], tuple(bd(jnp.where(lm, n, zero16)) for n in nsplit[ci])) for ci in chunks]
        xs = [xs[ci] - pp(_split(ts[ci], 2), tuple(bd(t) for t in xsplit[ci])) for ci in chunks]
    xsplit = [_split(x, 2) for x in xs]
    us = [pp(xsplit[ci], split_bd(bexps[ci] * vs[ci])) for ci in chunks]
    ws = [pp(xsplit[ci], split_bd(bexps[ci] * egams[ci] * ks[ci])) for ci in chunks]
    prepped = [(us[ci], ws[ci], qkms[ci], qs[ci] * egams[ci], kdecs[ci], egams[ci][c - 1:c, :],
                a_ref[ci * c:(ci + 1) * c, w3:w3 + DN_WIDTH]) for ci in chunks]

    s = s_scr[...]
    for ci, (u, w, qkm, qdec, kdec, glrow, gate) in enumerate(prepped):
        s16 = s.astype(BF16)
        vnew = u - _dot(w.astype(BF16), s16)
        v16 = vnew.astype(BF16)
        o = _dot(qdec.astype(BF16), s16) + _dot(qkm.astype(BF16), bd(v16))
        s = glrow * s + jnp.where(bdmask, _dot_tn(kdec.astype(BF16), v16), 0.0)
        ms = _dot_r01(o * o, bd1, 2) * (1.0 / DN_HEAD_DIM)
        o = o * lax.rsqrt(ms + EPS) * dnn_ref[...] * _silu(gate)
        o_ref[ci * c:(ci + 1) * c, :] = o.astype(BF16)
    s_scr[...] = s
    sout_ref[...] = s


def _delta(a, ab, cw, alog, dtb, dnn, e2, lmat, bd1, conv0, s0, bsz, t_pad, t_valid, tb):
    nblk = t_pad // tb
    nv_last = t_valid - (nblk - 1) * tb
    full = lambda x: pl.BlockSpec(x.shape, lambda b, j: (0,) * x.ndim)
    per_b = lambda x: pl.BlockSpec((None,) + x.shape[1:], lambda b, j: (b,) + (0,) * (x.ndim - 1))
    return pl.pallas_call(
        functools.partial(_delta_kernel, tb=tb, nv_last=nv_last, mask_rows=(t_valid != t_pad)),
        grid=(bsz, nblk),
        in_specs=[pl.BlockSpec((tb, A_COLS), lambda b, j: (b * nblk + j, 0)),
                  pl.BlockSpec((tb, LANES), lambda b, j: (b * nblk + j, 0)),
                  full(cw), full(alog), full(dtb), full(dnn), full(e2), full(lmat), full(bd1),
                  per_b(conv0), per_b(s0)],
        out_specs=(pl.BlockSpec((tb, DN_WIDTH), lambda b, j: (b * nblk + j, 0)), per_b(conv0), per_b(s0)),
        out_shape=(jax.ShapeDtypeStruct((bsz * t_pad, DN_WIDTH), BF16),
                   jax.ShapeDtypeStruct(conv0.shape, F32),
                   jax.ShapeDtypeStruct(s0.shape, F32)),
        scratch_shapes=[pltpu.VMEM((CONV_PAD + tb, 3 * DN_WIDTH), F32), pltpu.VMEM((DN_PACK, DN_PACK), F32)],
        compiler_params=_cparams(("parallel", "arbitrary")),
        name="gated_delta",
    )(a, ab, cw, alog, dtb, dnn, e2, lmat, bd1, conv0, s0)


def _delta_step_kernel(a_ref, ab_ref, cw_ref, alog_ref, dtb_ref, dnn_ref, e2_ref, bd1_ref, conv_ref, s_ref,
                       o_ref, convn_ref, sout_ref):
    nseq = a_ref.shape[0]
    w3 = 3 * DN_WIDTH
    x = a_ref[:, 0:w3]
    conv = cw_ref[DN_CONV - 1:DN_CONV, :] * x
    for i in range(DN_CONV - 1):
        conv = conv + cw_ref[i:i + 1, :] * conv_ref[i]
        if i > 0:
            convn_ref[i - 1] = conv_ref[i]
    convn_ref[DN_CONV - 2] = x
    r = _silu(conv)
    bd1 = bd1_ref[...]
    q = r[:, 0:DN_WIDTH]
    k = r[:, DN_WIDTH:2 * DN_WIDTH]
    v = r[:, 2 * DN_WIDTH:w3]
    q = q * lax.rsqrt(_dot_r01(q * q, bd1, 2) + EPS) * (DN_HEAD_DIM ** -0.5)
    k = k * lax.rsqrt(_dot_r01(k * k, bd1, 2) + EPS)
    abv = ab_ref[...]
    lane8 = lax.broadcasted_iota(jnp.int32, abv.shape, 1)
    g_all = -jnp.exp(alog_ref[...]) * _softplus(abv + dtb_ref[...])
    gbx = _dot_r01(jnp.where(lane8 < DN_HEADS, g_all, _sigmoid(abv)), e2_ref[...], 3)
    eg = jnp.exp(gbx[:, 0:DN_PACK])
    bexp = gbx[:, DN_PACK:]
    qk = _dot_r01(q * k, bd1, 2)
    wk = bexp * eg * k
    qd = q * eg
    row = lax.broadcasted_iota(jnp.int32, (nseq, DN_PACK), 0)
    bdmask = (lax.broadcasted_iota(jnp.int32, (DN_PACK, DN_PACK), 0) // DN_HEAD_DIM
              == lax.broadcasted_iota(jnp.int32, (DN_PACK, DN_PACK), 1) // DN_HEAD_DIM)

    def only(s, x):
        return jnp.where(row == s, x, 0.0).astype(BF16)

    s16 = [s_ref[s].astype(BF16) for s in range(nseq)]
    ws = sum(_dot(only(s, wk), s16[s]) for s in range(nseq))
    qs = sum(_dot(only(s, qd), s16[s]) for s in range(nseq))
    vnew = bexp * v - ws
    o = qs + qk * vnew
    ms = _dot_r01(o * o, bd1, 2) * (1.0 / DN_HEAD_DIM)
    o = o * lax.rsqrt(ms + EPS) * dnn_ref[...] * _silu(a_ref[:, w3:w3 + DN_WIDTH])
    o_ref[...] = o.astype(BF16)
    v16 = vnew.astype(BF16)
    for s in range(nseq):
        outer = _dot_tn(only(s, k), v16)
        sout_ref[s] = eg[s:s + 1, :] * s_ref[s] + jnp.where(bdmask, outer, 0.0)


def _delta_step(a, ab, cw, alog, dtb, dnn, e2, bd1, conv_t, s0):
    nseq = a.shape[0]
    blk = SUBLANES
    full = lambda x: pl.BlockSpec(x.shape, lambda i: (0,) * x.ndim)
    conv_spec = pl.BlockSpec((DN_CONV - 1, blk, 3 * DN_WIDTH), lambda i: (0, i, 0))
    s_spec = pl.BlockSpec((blk, DN_PACK, DN_PACK), lambda i: (i, 0, 0))
    return pl.pallas_call(
        _delta_step_kernel,
        grid=(nseq // blk,),
        in_specs=[pl.BlockSpec((blk, A_COLS), lambda i: (i, 0)), pl.BlockSpec((blk, LANES), lambda i: (i, 0)),
                  full(cw), full(alog), full(dtb), full(dnn), full(e2), full(bd1), conv_spec, s_spec],
        out_specs=(pl.BlockSpec((blk, DN_WIDTH), lambda i: (i, 0)), conv_spec, s_spec),
        out_shape=(jax.ShapeDtypeStruct((nseq, DN_WIDTH), BF16),
                   jax.ShapeDtypeStruct(conv_t.shape, F32),
                   jax.ShapeDtypeStruct(s0.shape, F32)),
        compiler_params=_cparams(("parallel",)),
        name="gated_delta_step",
    )(a, ab, cw, alog, dtb, dnn, e2, bd1, conv_t, s0)


def _delta_constants():
    c = DN_CHUNK
    lane = jnp.arange(2 * DN_PACK)
    row = jnp.arange(LANES)[:, None]
    e2 = (((lane[None, :] < DN_PACK) & (row == lane[None, :] // c))
          | ((lane[None, :] >= DN_PACK) & (row == DN_HEADS + (lane[None, :] - DN_PACK) // c))).astype(BF16)
    i = jnp.arange(c)[:, None]
    m = jnp.arange(c)[None, :]
    lmat = jnp.concatenate([(m <= i), (m > i)], axis=0).astype(BF16)
    hh = jnp.arange(DN_PACK) // c
    bd1 = (hh[:, None] == hh[None, :]).astype(BF16)
    return e2, lmat, bd1


def _tri_incl(n):
    return (jnp.arange(n)[:, None] >= jnp.arange(n)[None, :]).astype(BF16)


def _seg_matrices():
    c = jnp.arange(2 * SB_WIDTH)[:, None] // SB_HEAD_DIM
    j = jnp.arange(LANES)[None, :]
    seg = (c == j).astype(BF16)
    return seg, seg.T


def _prep_w_in(w):
    n_ab = 2 * DN_HEADS
    c1 = A_COLS
    c2 = c1 + n_ab
    c3 = c2 + B_COLS
    pad = jnp.zeros((D_MODEL, LANES - n_ab), w.dtype)
    return jnp.concatenate([w[:, :c1], w[:, c2:c3], w[:, c3:], w[:, c1:c2], pad], axis=1).astype(BF16)


def _row(v, width):
    return jnp.pad(v.astype(F32), (0, width - v.shape[0]))[None]


def _pack_state(s):
    eye = jnp.eye(DN_HEADS, dtype=s.dtype)
    return jnp.einsum('bhkv,hg->bhkgv', s, eye).reshape(s.shape[0], DN_PACK, DN_PACK)


def _unpack_state(sp):
    b = sp.shape[0]
    sp = sp.reshape(b, DN_HEADS, DN_HEAD_DIM, DN_HEADS, DN_HEAD_DIM)
    return jnp.stack([sp[:, h, :, h, :] for h in range(DN_HEADS)], axis=1)


TM_PROJ = 256
TM_FFN = 512
TH_FFN = 6 * MXU_TILE
TB_DELTA = 4 * DN_CHUNK
TQ_ATTN = 256
S5_ROWS = 1024


def _mixers(x, lw, consts, bsz, t, conv0, s0, x0_ssm, attn_fn, layer, kv_bufs):
    (w16, qkn, cw, alog, dtb, dnn, wb, lam, cm, dskip, gw16, gb, wo16, ffn_g, wg16, wu16, wd16, attn_g) = lw
    seg, segt, e2, lmat, bd1 = consts
    n = bsz * t
    tm = min(TM_PROJ, n)
    a, ab, q16, k16, v16, kf, vf, u, u_time_major = _proj_in(x, attn_g, w16, qkn, seg, segt, tm, bsz, t,
                                                             layer, kv_bufs)

    hist = CONV_PAD - (DN_CONV - 1)
    if t == 1 and bsz % SUBLANES == 0:
        o_a, conv_t, s_new = _delta_step(a, ab, cw, alog, dtb, dnn, e2, bd1,
                                         conv0[:, hist:].transpose(1, 0, 2), s0)
        convn = jnp.pad(conv_t.transpose(1, 0, 2), ((0, 0), (hist, 0), (0, 0)))
    else:
        tb = min(TB_DELTA, -(-t // DN_CHUNK) * DN_CHUNK)
        t_pad = -(-t // tb) * tb
        if t_pad != t:
            padt = lambda y: jnp.pad(y.reshape(bsz, t, -1), ((0, 0), (0, t_pad - t), (0, 0))).reshape(bsz * t_pad, -1)
            a_in, ab_in = padt(a), padt(ab)
        else:
            a_in, ab_in = a, ab
        o_a, convn, s_new = _delta(a_in, ab_in, cw, alog, dtb, dnn, e2, lmat, bd1, conv0, s0, bsz, t_pad, t, tb)
        if t_pad != t:
            o_a = o_a.reshape(bsz, t_pad, DN_WIDTH)[:, :t].reshape(n, DN_WIDTH)

    o_b = attn_fn(q16, k16, v16)

    nb = bsz if (bsz * 2 == SUBLANES and t % 2 == 0) else -(-bsz // SUBLANES) * SUBLANES
    steps = min(S5_ROWS // nb, t)
    tm_ffn = min(TM_FFN, n)
    u_tb = u.reshape(t, bsz, SSM_WIDTH) if u_time_major else u.reshape(bsz, t, SSM_WIDTH).transpose(1, 0, 2)
    if nb != bsz:
        u_tb = jnp.pad(u_tb, ((0, 0), (0, nb - bsz), (0, 0)))
    o_c, xt = _s5(u_tb.reshape(t * nb, SSM_WIDTH), wb, lam, cm, dskip, gw16, gb,
                  jnp.pad(x0_ssm, ((0, nb - bsz), (0, 0))), steps, nb)
    direct = u_time_major and nb == bsz and t % tm_ffn == 0
    if not direct:
        o_c = o_c.reshape(t, nb, SSM_WIDTH)[:, :bsz].transpose(1, 0, 2).reshape(n, SSM_WIDTH)

    x = _out_ffn(x, o_a, o_b, o_c, wo16, ffn_g, wg16, wu16, wd16, tm_ffn, TH_FFN, t if direct else None)
    leaves = (convn[:, CONV_PAD - (DN_CONV - 1):], _unpack_state(s_new),
              xt[:bsz, :SSM_FLAT].reshape(bsz, SSM_GROUPS, SSM_STATE),
              xt[:bsz, SSM_FLAT:].reshape(bsz, SSM_GROUPS, SSM_STATE))
    return x, (kf, vf), leaves


def kernel(x_prompt, x_sample, cache_k, cache_v, page_table, state_conv, state_delta, state_ssm_re, state_ssm_im,
           attn_norm, w_in, conv_w, dn_a_log, dn_dt_bias, dn_out_norm, sb_q_norm, sb_k_norm, sb_bias,
           ssm_lambda_re, ssm_lambda_im, ssm_log_step, ssm_b_re, ssm_b_im, ssm_c_re, ssm_c_im, ssm_d,
           ssm_glu_w, ssm_glu_b, w_out, ffn_norm, w_gate, w_up, w_down):
    bp, tp, _ = x_prompt.shape
    bs, ts, _ = x_sample.shape
    assert ts == 1, "the sample group decodes one token per sequence"
    n_pool = cache_k.shape[1]
    ck = cache_k.transpose(0, 1, 3, 4, 2).reshape(DEPTH, n_pool, SB_WIDTH, PAGE_SIZE)
    cv = cache_v.transpose(0, 1, 3, 4, 2).reshape(DEPTH, n_pool, SB_WIDTH, PAGE_SIZE)
    consts = _seg_matrices() + _delta_constants()
    tri_p = _tri_incl(TQ_ATTN)
    tri_d = _tri_incl(PAGE_SIZE)

    xp = x_prompt.reshape(bp * tp, D_MODEL)
    xs = x_sample.reshape(bs * ts, D_MODEL)
    conv0_p = jnp.zeros((bp, CONV_PAD, 3 * DN_WIDTH), F32)
    s0_p = jnp.zeros((bp, DN_PACK, DN_PACK), F32)
    x0_p = jnp.zeros((bp, 2 * SSM_FLAT), F32)
    acc = [[] for _ in range(8)]
    kv_p = kv_s = None
    for l in range(DEPTH):
        wb, cm, lam = _prep_s5(ssm_lambda_re[l], ssm_lambda_im[l], ssm_log_step[l],
                               ssm_b_re[l], ssm_b_im[l], ssm_c_re[l], ssm_c_im[l])
        qkn = jnp.concatenate([jnp.tile(sb_q_norm[l], SB_HEADS), jnp.tile(sb_k_norm[l], SB_HEADS)])[None]
        lw = (_prep_w_in(w_in[l]), qkn, conv_w[l], _row(dn_a_log[l], LANES), _row(dn_dt_bias[l], LANES),
              jnp.tile(dn_out_norm[l], DN_HEADS)[None], wb, lam, cm, ssm_d[l][None],
              ssm_glu_w[l].astype(BF16), ssm_glu_b[l][None], w_out[l].astype(BF16), ffn_norm[l][None],
              w_gate[l].astype(BF16), w_up[l].astype(BF16), w_down[l].astype(BF16), attn_norm[l][None])
        bias = sb_bias[l].astype(F32)

        attn_p = lambda q, k, v: _attn_prompt(q, k, v, bias, tri_p, bp, tp, TQ_ATTN)
        xp, kv_p, st_p = _mixers(xp, lw, consts, bp, tp, conv0_p, s0_p, x0_p, attn_p, l, kv_p)

        bias_col = jnp.broadcast_to(bias[:, None], (SB_HEADS, LANES))
        attn_s = lambda q, k, v: _attn_decode(q.reshape(bs, 1, SB_WIDTH), bias_col, tri_d, page_table,
                                              ck, cv, l).reshape(bs, SB_WIDTH)
        conv0_s = jnp.pad(state_conv[l], ((0, 0), (CONV_PAD - (DN_CONV - 1), 0), (0, 0)))
        x0_s = jnp.concatenate([state_ssm_re[l].reshape(bs, SSM_FLAT), state_ssm_im[l].reshape(bs, SSM_FLAT)], axis=1)
        xs, kv_s, st_s = _mixers(xs, lw, consts, bs, ts, conv0_s, _pack_state(state_delta[l]), x0_s, attn_s, l, kv_s)
        for a, val in zip(acc, st_p + st_s):
            a.append(val)
    st = [jnp.stack(a, axis=0) for a in acc]
    def heads(buf, b, t):
        if buf.ndim == 4:
            return buf.reshape(DEPTH, b, SB_HEADS, SB_HEAD_DIM, t).transpose(0, 1, 4, 2, 3)
        return buf.reshape(DEPTH, b, t, SB_HEADS, SB_HEAD_DIM)

    return (xp.reshape(bp, tp, D_MODEL), xs.reshape(bs, ts, D_MODEL),
            heads(kv_p[0], bp, tp), heads(kv_p[1], bp, tp), *st[:4],
            heads(kv_s[0], bs, ts), heads(kv_s[1], bs, ts), *st[4:])
```

```python
import functools
import math

import jax
import jax.numpy as jnp
from jax import lax
from jax.experimental import pallas as pl
from jax.experimental.pallas import tpu as pltpu

F32 = jnp.float32
BF16 = jnp.bfloat16

D_MODEL = 1024
DEPTH = 4
PAGE_SIZE = 128
DN_HEADS = 4
DN_HEAD_DIM = 64
DN_WIDTH = DN_HEADS * DN_HEAD_DIM
DN_CONV = 4
DN_CHUNK = 64
SB_HEADS = 8
SB_HEAD_DIM = 64
SB_WIDTH = SB_HEADS * SB_HEAD_DIM
SSM_WIDTH = D_MODEL - DN_WIDTH - SB_WIDTH
SSM_GROUP = 16
SSM_GROUPS = SSM_WIDTH // SSM_GROUP
SSM_STATE = 64
SSM_FLAT = SSM_GROUPS * SSM_STATE
FFN_HIDDEN = 2816
EPS = 1e-6

LANES = 128
SUBLANES = 8
VMEM_LIMIT = 56 * 1024 * 1024

A_COLS = 4 * DN_WIDTH
B_COLS = 3 * SB_WIDTH
IN_PAD = A_COLS + B_COLS + SSM_WIDTH + LANES


def _cparams(sem):
    return pltpu.CompilerParams(dimension_semantics=sem, vmem_limit_bytes=VMEM_LIMIT)


def _split(x, n):
    terms = []
    r = x
    for i in range(n):
        t = r.astype(BF16)
        terms.append(t)
        if i + 1 < n:
            r = r - t.astype(F32)
    return terms


def _dot(a, b):
    return jnp.dot(a, b, preferred_element_type=F32)


def _dot_l01(a01, x, n):
    return sum(_dot(a01, t) for t in _split(x, n))


def _dot_r01(x, b01, n):
    return sum(_dot(t, b01) for t in _split(x, n))


def _sigmoid(x):
    return 1.0 / (1.0 + jnp.exp(-x))


def _silu(x):
    return x * _sigmoid(x)


LOG2E = 1.4426950408889634


def _softplus(x):
    return jnp.maximum(x, 0.0) + jnp.log(1.0 + jnp.exp2(jnp.abs(x) * (-LOG2E)))


def _proj_in_kernel(x_ref, g_ref, w_ref, qkn_ref, seg_ref, segt_ref, *refs, feature_major):
    a_ref, ab_ref, q16_ref, k16_ref, v16_ref, kf_ref, vf_ref, u_ref = refs[-8:]
    x = x_ref[...]
    ms = jnp.mean(x * x, axis=-1, keepdims=True)
    h = (x * lax.rsqrt(ms + EPS) * g_ref[...]).astype(BF16)
    a_ref[...] = _dot(h, w_ref[:, 0:A_COLS])
    c0 = A_COLS
    qk = _dot(h, w_ref[:, c0:c0 + 2 * SB_WIDTH])
    v = _dot(h, w_ref[:, c0 + 2 * SB_WIDTH:c0 + B_COLS])
    c0 += B_COLS
    u_ref[...] = _dot(h, w_ref[:, c0:c0 + SSM_WIDTH])
    c0 += SSM_WIDTH
    ab_ref[...] = _dot(h, w_ref[:, c0:c0 + LANES])
    ssq = _dot_r01(qk * qk, seg_ref[...], 1)
    inv = lax.rsqrt(ssq * (1.0 / SB_HEAD_DIM) + EPS)
    qkn = qk * _dot_r01(inv, segt_ref[...], 2) * qkn_ref[...]
    qn = qkn[:, 0:SB_WIDTH]
    kn = qkn[:, SB_WIDTH:]
    q16_ref[...] = (qn * (SB_HEAD_DIM ** -0.5)).astype(BF16)
    k16_ref[...] = kn.astype(BF16)
    v16_ref[...] = v.astype(BF16)
    if feature_major:
        kf_ref[...] = kn.T
        vf_ref[...] = v.T
    else:
        kf_ref[...] = kn
        vf_ref[...] = v


def _proj_in(x, gain, w16, qkn, seg, segt, tm, bsz, t, layer, kv_bufs):
    n = x.shape[0]
    row = lambda w: pl.BlockSpec((tm, w), lambda i: (i, 0))
    full = lambda a: pl.BlockSpec(a.shape, lambda i: (0,) * a.ndim)
    feature_major = t % tm == 0 and tm % LANES == 0
    if feature_major:
        nt = t // tm
        slab = pl.BlockSpec((None, None, SB_WIDTH, tm), lambda i: (layer, i // nt, 0, i % nt))
        kv_shape = jax.ShapeDtypeStruct((DEPTH, bsz, SB_WIDTH, t), F32)
        u_spec = pl.BlockSpec((tm, SSM_WIDTH), lambda i: (i % nt, i // nt))
        u_shape = jax.ShapeDtypeStruct((t, bsz * SSM_WIDTH), F32)
    else:
        slab = pl.BlockSpec((None, tm, SB_WIDTH), lambda i: (layer, i, 0))
        kv_shape = jax.ShapeDtypeStruct((DEPTH, n, SB_WIDTH), F32)
        u_spec = row(SSM_WIDTH)
        u_shape = jax.ShapeDtypeStruct((n, SSM_WIDTH), F32)
    out_shapes = (
        jax.ShapeDtypeStruct((n, A_COLS), F32),
        jax.ShapeDtypeStruct((n, LANES), F32),
        jax.ShapeDtypeStruct((n, SB_WIDTH), BF16),
        jax.ShapeDtypeStruct((n, SB_WIDTH), BF16),
        jax.ShapeDtypeStruct((n, SB_WIDTH), BF16),
        kv_shape,
        kv_shape,
        u_shape,
    )
    in_specs = [row(D_MODEL), full(gain), full(w16), full(qkn), full(seg), full(segt)]
    args = [x, gain, w16, qkn, seg, segt]
    aliases = {}
    if kv_bufs is not None:
        in_specs += [pl.BlockSpec(memory_space=pl.ANY)] * 2
        args += list(kv_bufs)
        aliases = {len(args) - 2: 5, len(args) - 1: 6}
    outs = pl.pallas_call(
        functools.partial(_proj_in_kernel, feature_major=feature_major),
        grid=(n // tm,),
        in_specs=in_specs,
        out_specs=(row(A_COLS), row(LANES), row(SB_WIDTH), row(SB_WIDTH), row(SB_WIDTH),
                   slab, slab, u_spec),
        out_shape=out_shapes,
        input_output_aliases=aliases,
        compiler_params=_cparams(("parallel",)),
        name="proj_in",
    )(*args)
    return outs[:7] + (outs[7].reshape(n, SSM_WIDTH), feature_major)


MXU_TILE = 256


def _hidden_splits(th_max):
    assert FFN_HIDDEN % MXU_TILE == 0
    tiles = FFN_HIDDEN // MXU_TILE
    per = max(th_max // MXU_TILE, 1)
    cuts = list(range(0, tiles, per)) + [tiles]
    return [(lo * MXU_TILE, hi * MXU_TILE) for lo, hi in zip(cuts[:-1], cuts[1:])]


def _out_ffn_kernel(x_ref, oa_ref, ob_ref, oc_ref, wo_ref, g_ref, wg_ref, wu_ref, wd_ref, y_ref, *, th):
    mix = (_dot(oa_ref[...], wo_ref[0:DN_WIDTH, :])
           + _dot(ob_ref[...], wo_ref[DN_WIDTH:DN_WIDTH + SB_WIDTH, :])
           + _dot(oc_ref[...], wo_ref[DN_WIDTH + SB_WIDTH:, :]))
    x1 = x_ref[...] + mix
    ms = jnp.mean(x1 * x1, axis=-1, keepdims=True)
    h = (x1 * lax.rsqrt(ms + EPS) * g_ref[...]).astype(BF16)
    y = x1
    for lo, hi in _hidden_splits(th):
        gate = _dot(h, wg_ref[:, lo:hi])
        up = _dot(h, wu_ref[:, lo:hi])
        y = y + _dot((_silu(gate) * up).astype(BF16), wd_ref[lo:hi, :])
    y_ref[...] = y


def _out_ffn(x, oa, ob, oc, wo16, gain, wg16, wu16, wd16, tm, th, oc_time_major_t=None):
    n = x.shape[0]
    row = lambda w: pl.BlockSpec((tm, w), lambda i: (i, 0))
    fixed = lambda a: pl.BlockSpec(a.shape, lambda i: (0,) * a.ndim, pipeline_mode=pl.Buffered(1))
    oc_spec = row(SSM_WIDTH)
    if oc_time_major_t is not None:
        t = oc_time_major_t
        nt = t // tm
        oc = oc.reshape(t, (n // t) * SSM_WIDTH)
        oc_spec = pl.BlockSpec((tm, SSM_WIDTH), lambda i: (i % nt, i // nt))
    return pl.pallas_call(
        functools.partial(_out_ffn_kernel, th=th),
        grid=(n // tm,),
        in_specs=[row(D_MODEL), row(DN_WIDTH), row(SB_WIDTH), oc_spec,
                  fixed(wo16), fixed(gain), fixed(wg16), fixed(wu16), fixed(wd16)],
        out_specs=row(D_MODEL),
        out_shape=jax.ShapeDtypeStruct((n, D_MODEL), F32),
        compiler_params=_cparams(("parallel",)),
        name="out_ffn",
    )(x, oa, ob, oc, wo16, gain, wg16, wu16, wd16)


def _gelu_tanh(x):
    c = math.sqrt(2.0 / math.pi)
    return 0.5 * x * (1.0 + jnp.tanh(c * (x + 0.044715 * (x * x * x))))


def _s5_kernel(u_ref, wb_ref, lam_ref, cm_ref, d_ref, gw_ref, gb_ref, x0_ref,
               o_ref, xt_ref, bu_scr, x_scr, *, steps, nb):
    two_step = nb * 2 == SUBLANES
    tile = SUBLANES if two_step else nb

    @pl.when(pl.program_id(0) == 0)
    def _():
        x_scr[...] = jnp.concatenate([x0_ref[...]] * 2, axis=0) if two_step else x0_ref[...]

    u = u_ref[...]
    bu_scr[...] = _dot(u.astype(BF16), wb_ref[...])
    lre = jnp.broadcast_to(lam_ref[:, 0:SSM_FLAT], (tile, SSM_FLAT))
    lim = jnp.broadcast_to(lam_ref[:, SSM_FLAT:], (tile, SSM_FLAT))
    lower = lax.broadcasted_iota(jnp.int32, (tile, 2 * SSM_FLAT), 0) < nb

    def advance(x, bu):
        xr = x[:, 0:SSM_FLAT]
        xi = x[:, SSM_FLAT:]
        nr = lre * xr - lim * xi + bu[:, 0:SSM_FLAT]
        ni = lre * xi + lim * xr + bu[:, SSM_FLAT:]
        return jnp.concatenate([nr, ni], axis=1)

    def step(t, x):
        r0 = pl.multiple_of(t * tile, tile)
        bu = bu_scr[pl.ds(r0, tile), :]
        x1 = advance(x, bu)
        if two_step:
            x2 = advance(pltpu.roll(x1, nb, 0), bu)
            bu_scr[pl.ds(r0, tile), :] = jnp.where(lower, x1, x2)
            return jnp.where(lower, pltpu.roll(x2, nb, 0), x2)
        bu_scr[pl.ds(r0, tile), :] = x1
        return x1

    x = lax.fori_loop(0, steps * nb // tile, step, x_scr[...])
    x_scr[...] = x
    xt_ref[...] = x[0:nb, :]
    y = _dot(bu_scr[...].astype(BF16), cm_ref[...]) + d_ref[...] * u
    z = _gelu_tanh(y)
    o_ref[...] = (z * _sigmoid(_dot(z.astype(BF16), gw_ref[...]) + gb_ref[...])).astype(BF16)


def _s5(u_rows, wb, lam, cm, d, gw16, gb, x0, steps, nb):
    assert nb % SUBLANES == 0 or (nb * 2 == SUBLANES and steps % 2 == 0)
    n = u_rows.shape[0]
    rows = steps * nb
    full = lambda a: pl.BlockSpec(a.shape, lambda i: (0,) * a.ndim)
    return pl.pallas_call(
        functools.partial(_s5_kernel, steps=steps, nb=nb),
        grid=(n // rows,),
        in_specs=[pl.BlockSpec((rows, SSM_WIDTH), lambda i: (i, 0)),
                  full(wb), full(lam), full(cm), full(d), full(gw16), full(gb), full(x0)],
        out_specs=(pl.BlockSpec((rows, SSM_WIDTH), lambda i: (i, 0)), full(x0)),
        out_shape=(jax.ShapeDtypeStruct((n, SSM_WIDTH), BF16),
                   jax.ShapeDtypeStruct(x0.shape, F32)),
        scratch_shapes=[pltpu.VMEM((rows, 2 * SSM_FLAT), F32),
                        pltpu.VMEM((max(nb, SUBLANES), 2 * SSM_FLAT), F32)],
        compiler_params=_cparams(("arbitrary",)),
        name="s5_scan",
    )(u_rows, wb, lam, cm, d, gw16, gb, x0)


def _prep_s5(lam_re, lam_im, log_step, b_re, b_im, c_re, c_im):
    step = jnp.exp(log_step)[:, None]
    mag = jnp.exp(lam_re * step)
    ang = lam_im * step
    lb_re, lb_im = mag * jnp.cos(ang), mag * jnp.sin(ang)
    den = lam_re * lam_re + lam_im * lam_im
    f_re = ((lb_re - 1.0) * lam_re + lb_im * lam_im) / den
    f_im = (lb_im * lam_re - (lb_re - 1.0) * lam_im) / den
    bb_re = f_re[..., None] * b_re - f_im[..., None] * b_im
    bb_im = f_re[..., None] * b_im + f_im[..., None] * b_re
    eye = jnp.eye(SSM_GROUPS, dtype=F32)
    pack_b = lambda bb: jnp.einsum('gpc,gh->gchp', bb, eye).reshape(SSM_WIDTH, SSM_FLAT)
    pack_c = lambda cc: jnp.einsum('gcp,gh->gphc', cc, eye).reshape(SSM_FLAT, SSM_WIDTH)
    wb = jnp.concatenate([pack_b(bb_re), pack_b(bb_im)], axis=1).astype(BF16)
    cm = jnp.concatenate([pack_c(c_re), -pack_c(c_im)], axis=0).astype(BF16)
    lam = jnp.concatenate([lb_re.reshape(1, SSM_FLAT), lb_im.reshape(1, SSM_FLAT)], axis=1)
    return wb, cm, lam


HEAD_GROUP = 4
GROUP_W = HEAD_GROUP * SB_HEAD_DIM


def _dot_nt(a, b):
    return lax.dot_general(a, b, (((1,), (1,)), ((), ())), preferred_element_type=F32)


def _attn_kernel(bias_ref, q_ref, k_ref, v_ref, tri_ref, o_ref, acc_scr, z_scr, w_scr, vs_scr, *, tq):
    g = pl.program_id(1)
    i = pl.program_id(2)
    q = q_ref[...]
    lane_head = lax.broadcasted_iota(jnp.int32, (tq, GROUP_W), 1) // SB_HEAD_DIM

    @pl.when(i == 0)
    def _():
        def fill(j, _):
            vb = v_ref[pl.ds(pl.multiple_of(j * tq, tq), tq), :]
            for hh in range(HEAD_GROUP):
                vs_scr[j, hh * tq:(hh + 1) * tq, :] = jnp.where(lane_head == hh, vb, jnp.zeros_like(vb))
            return 0
        lax.fori_loop(0, vs_scr.shape[0], fill, 0)
    causal = (lax.broadcasted_iota(jnp.int32, (tq, tq), 1)
              < lax.broadcasted_iota(jnp.int32, (tq, tq), 0))
    tri = tri_ref[...]
    heads = range(HEAD_GROUP)
    qms = [jnp.where(lane_head == hh, q, jnp.zeros_like(q)) for hh in heads]
    biases = [bias_ref[g * HEAD_GROUP + hh] for hh in heads]

    r0 = pl.multiple_of(i * tq, tq)

    def suffix_sum(sp):
        return _dot(sp.astype(BF16), tri)

    def scores(c0):
        kb = k_ref[pl.ds(c0, tq), :]
        return [_dot_nt(qms[hh], kb) + biases[hh] for hh in heads]

    zs = scores(r0)
    cs = [suffix_sum(jnp.where(causal, _softplus(zs[hh]), 0.0)) for hh in heads]
    for hh in heads:
        w_scr[:, hh * tq:(hh + 1) * tq] = jnp.where(causal, jnp.exp(zs[hh] - cs[hh]), 0.0).astype(BF16)
    acc_scr[...] = jnp.zeros((tq, GROUP_W), F32)
    zn = scores(pl.multiple_of(jnp.maximum(i - 1, 0) * tq, tq))
    for hh in heads:
        z_scr[hh] = zn[hh]
    carries = tuple(cb[:, 0:1] for cb in cs)

    def body(jj, carries):
        zc = [z_scr[hh] for hh in heads]
        zn = scores(pl.multiple_of(jnp.maximum(i - jj - 1, 0) * tq, tq))
        acc_scr[...] += _dot(w_scr[...], vs_scr[i - jj + 1])
        cs = [suffix_sum(_softplus(zc[hh])) + carries[hh] for hh in heads]
        for hh in heads:
            w_scr[:, hh * tq:(hh + 1) * tq] = jnp.exp(zc[hh] - cs[hh]).astype(BF16)
            z_scr[hh] = zn[hh]
        return tuple(cb[:, 0:1] for cb in cs)

    lax.fori_loop(1, i + 1, body, carries)
    o_ref[...] = (acc_scr[...] + _dot(w_scr[...], vs_scr[0])).astype(BF16)


def _attn_prompt(q16, k16, v16, bias, tri, bsz, t, tq):
    nq = t // tq
    ngrp = SB_WIDTH // GROUP_W
    grid_spec = pltpu.PrefetchScalarGridSpec(
        num_scalar_prefetch=1,
        grid=(bsz, ngrp, nq),
        in_specs=[pl.BlockSpec((tq, GROUP_W), lambda b, g, i, s: (b * nq + i, g)),
                  pl.BlockSpec((t, GROUP_W), lambda b, g, i, s: (b, g)),
                  pl.BlockSpec((t, GROUP_W), lambda b, g, i, s: (b, g)),
                  pl.BlockSpec((tq, tq), lambda b, g, i, s: (0, 0))],
        out_specs=pl.BlockSpec((tq, GROUP_W), lambda b, g, i, s: (b * nq + i, g)),
        scratch_shapes=[pltpu.VMEM((tq, GROUP_W), F32),
                        pltpu.VMEM((HEAD_GROUP, tq, tq), F32),
                        pltpu.VMEM((tq, HEAD_GROUP * tq), BF16),
                        pltpu.VMEM((nq, HEAD_GROUP * tq, GROUP_W), BF16)],
    )
    return pl.pallas_call(
        functools.partial(_attn_kernel, tq=tq),
        grid_spec=grid_spec,
        out_shape=jax.ShapeDtypeStruct((bsz * t, SB_WIDTH), BF16),
        compiler_params=_cparams(("arbitrary", "arbitrary", "arbitrary")),
        name="sb_attn_prompt",
    )(bias, q16, k16, v16, tri)


PAGES_PER_STEP = 16


def _attn_decode_kernel(pt_ref, q_ref, bias_ref, tri_ref, *refs, pp):
    del pt_ref
    k_refs = refs[:pp]
    v_refs = refs[pp:2 * pp]
    o_ref, acc_scr, carry_scr = refs[2 * pp:]
    j = pl.program_id(1)

    @pl.when(j == 0)
    def _():
        acc_scr[...] = jnp.zeros_like(acc_scr)
        carry_scr[...] = jnp.zeros_like(carry_scr)

    on_diag = (lax.broadcasted_iota(jnp.int32, (SB_HEADS, SB_WIDTH), 0)
               == lax.broadcasted_iota(jnp.int32, (SB_HEADS, SB_WIDTH), 1) // SB_HEAD_DIM)
    q = jnp.broadcast_to(q_ref[0].astype(F32), (SB_HEADS, SB_WIDTH))
    qbd = jnp.where(on_diag, q, 0.0).astype(BF16)
    bias = bias_ref[...]
    tri = tri_ref[...]
    kcat = jnp.concatenate([k_refs[p][...].astype(BF16) for p in range(pp)], axis=1)
    zw = _dot(qbd, kcat)
    z = jnp.concatenate([zw[:, p * PAGE_SIZE:(p + 1) * PAGE_SIZE] + bias for p in range(pp)], axis=0)
    cum = _dot_r01(_softplus(z), tri, 2)
    carry = carry_scr[:, 0:1]
    offs = []
    for p in range(pp):
        offs.append(carry)
        carry = carry + cum[p * SB_HEADS:(p + 1) * SB_HEADS, 0:1]
    w = jnp.exp(z - cum - jnp.concatenate(offs, axis=0)).astype(BF16)
    wcat = jnp.concatenate([w[p * SB_HEADS:(p + 1) * SB_HEADS, :] for p in range(pp)], axis=1)
    vcat = jnp.concatenate([v_refs[p][...].astype(BF16) for p in range(pp)], axis=1)
    acc = acc_scr[...] + _dot_nt(wcat, vcat)
    acc_scr[...] = acc
    carry_scr[...] = jnp.broadcast_to(carry, carry_scr.shape)

    @pl.when(j == pl.num_programs(1) - 1)
    def _():
        o_ref[0] = jnp.sum(jnp.where(on_diag, acc, 0.0), axis=0, keepdims=True).astype(BF16)


def _attn_decode(q16, bias_col, tri, page_table, cache_k, cache_v, layer):
    nseq, npg = page_table.shape
    pp = PAGES_PER_STEP

    def page_spec(p):
        return pl.BlockSpec((None, None, SB_WIDTH, PAGE_SIZE),
                            lambda b, j, pt: (layer, pt[b, npg - 1 - (j * pp + p)], 0, 0))

    grid_spec = pltpu.PrefetchScalarGridSpec(
        num_scalar_prefetch=1,
        grid=(nseq, npg // pp),
        in_specs=[pl.BlockSpec((1, 1, SB_WIDTH), lambda b, j, pt: (b, 0, 0)),
                  pl.BlockSpec((SB_HEADS, LANES), lambda b, j, pt: (0, 0)),
                  pl.BlockSpec((PAGE_SIZE, PAGE_SIZE), lambda b, j, pt: (0, 0))]
                 + [page_spec(p) for p in range(pp)] * 2,
        out_specs=pl.BlockSpec((1, 1, SB_WIDTH), lambda b, j, pt: (b, 0, 0)),
        scratch_shapes=[pltpu.VMEM((SB_HEADS, SB_WIDTH), F32), pltpu.VMEM((SB_HEADS, LANES), F32)],
    )
    return pl.pallas_call(
        functools.partial(_attn_decode_kernel, pp=pp),
        grid_spec=grid_spec,
        out_shape=jax.ShapeDtypeStruct((nseq, 1, SB_WIDTH), BF16),
        compiler_params=_cparams(("parallel", "arbitrary")),
        name="sb_attn_decode",
    )(page_table, q16, bias_col, tri, *([cache_k] * pp), *([cache_v] * pp))


DN_PACK = DN_HEADS * DN_CHUNK
CONV_PAD = SUBLANES


def _dot_tn(a, b):
    return lax.dot_general(a, b, (((0,), (0,)), ((), ())), preferred_element_type=F32)


def _delta_kernel(a_ref, ab_ref, cw_ref, alog_ref, dtb_ref, dnn_ref, e2_ref, lmat_ref, bd1_ref,
                  conv0_ref, s0_ref, o_ref, convn_ref, sout_ref, xp_scr, s_scr,
                  *, tb, nv_last, mask_rows):
    c = DN_CHUNK
    w3 = 3 * DN_WIDTH
    j = pl.program_id(1)

    @pl.when(j == 0)
    def _():
        xp_scr[0:CONV_PAD, :] = conv0_ref[...]
        s_scr[...] = s0_ref[...]

    xp_scr[CONV_PAD:CONV_PAD + tb, :] = a_ref[:, 0:w3]
    off = CONV_PAD - (DN_CONV - 1)
    conv = cw_ref[0:1, :] * xp_scr[off:off + tb, :]
    for i in range(1, DN_CONV):
        conv = conv + cw_ref[i:i + 1, :] * xp_scr[off + i:off + i + tb, :]
    convn_ref[...] = xp_scr[nv_last:nv_last + CONV_PAD, :]
    xp_scr[0:CONV_PAD, :] = xp_scr[tb:tb + CONV_PAD, :]
    r = _silu(conv)

    ii = lax.broadcasted_iota(jnp.int32, (c, DN_PACK), 0)
    jj = lax.broadcasted_iota(jnp.int32, (c, DN_PACK), 1) % c
    strict = jj < ii
    incl = jj <= ii
    bdmask = (lax.broadcasted_iota(jnp.int32, (DN_PACK, DN_PACK), 0) // c
              == lax.broadcasted_iota(jnp.int32, (DN_PACK, DN_PACK), 1) // c)
    lane8 = lax.broadcasted_iota(jnp.int32, (c, LANES), 1)
    bd1 = bd1_ref[...]

    def bd(x16):
        return jnp.where(bdmask, jnp.concatenate([x16] * DN_HEADS, axis=0), jnp.zeros((), BF16))

    def pp(a2t, b2t):
        return _dot(a2t[0], b2t[0]) + _dot(a2t[0], b2t[1]) + _dot(a2t[1], b2t[0])

    def split_bd(x):
        return tuple(bd(t) for t in _split(x, 2))

    def level_mask(b):
        return (((ii // (2 * b)) == (jj // (2 * b))) & ((ii % (2 * b)) >= b) & ((jj % (2 * b)) < b))

    chunks = range(tb // c)
    eye = (ii == jj).astype(F32)
    levels = []
    b = 1
    while b < c:
        levels.append(level_mask(b))
        b *= 2
    qs, ks, vs, bexps, egams, kdecs, qkms, nmats = [], [], [], [], [], [], [], []
    for ci in chunks:
        rows = slice(ci * c, (ci + 1) * c)
        q = r[rows, 0:DN_WIDTH]
        k = r[rows, DN_WIDTH:2 * DN_WIDTH]
        v = r[rows, 2 * DN_WIDTH:w3]
        q = q * lax.rsqrt(_dot_r01(q * q, bd1, 1) + EPS) * (DN_HEAD_DIM ** -0.5)
        k = k * lax.rsqrt(_dot_r01(k * k, bd1, 1) + EPS)
        abv = ab_ref[rows, :]
        g_all = -jnp.exp(alog_ref[...]) * _softplus(abv + dtb_ref[...])
        gb = jnp.where(lane8 < DN_HEADS, g_all, _sigmoid(abv))
        if mask_rows:
            valid = lax.broadcasted_iota(jnp.int32, (c, 1), 0) < (nv_last - ci * c)
            q = jnp.where(valid, q, 0.0)
            k = jnp.where(valid, k, 0.0)
            v = jnp.where(valid, v, 0.0)
            gb = jnp.where(valid, gb, 0.0)
        gbx = _dot_r01(gb, e2_ref[...], 3)
        gexp = gbx[:, 0:DN_PACK]
        bexp = gbx[:, DN_PACK:]
        cums = _dot_l01(lmat_ref[...], gexp, 3)
        egam = jnp.exp(cums[0:c, :])
        kdec = k * jnp.exp(cums[c:, :])
        dmat = _dot_l01(lmat_ref[0:c, :], jnp.where(strict, gexp, 0.0), 3)
        decay = jnp.exp(jnp.where(incl, dmat, 0.0))
        k16 = k.astype(BF16)
        bdk = bd(k16)
        kk = _dot_nt(k16, bdk)
        qk = _dot_nt(q.astype(BF16), bdk)
        nmats.append(jnp.where(strict, bexp * kk * decay, 0.0))
        qkms.append(jnp.where(incl, qk * decay, 0.0))
        qs.append(q)
        ks.append(k)
        vs.append(v)
        bexps.append(bexp)
        egams.append(egam)
        kdecs.append(kdec)

    xs = [eye - jnp.where(levels[0], nm, 0.0) for nm in nmats]
    nsplit = [_split(nm, 2) for nm in nmats]
    zero16 = jnp.zeros((), BF16)
    for lm in levels[1:]:
        xsplit = [_split(x, 2) for x in xs]
        ts = [pp(xsplit[ci], tuple(bd(jnp.where(lm, n, zero16)) for n in nsplit[ci])) for ci in chunks]
        xs = [xs[ci] - pp(_split(ts[ci], 2), tuple(bd(t) for t in xsplit[ci])) for ci in chunks]
    xsplit = [_split(x, 2) for x in xs]
    us = [pp(xsplit[ci], split_bd(bexps[ci] * vs[ci])) for ci in chunks]
    ws = [pp(xsplit[ci], split_bd(bexps[ci] * egams[ci] * ks[ci])) for ci in chunks]
    prepped = [(us[ci], ws[ci], qkms[ci], qs[ci] * egams[ci], kdecs[ci], egams[ci][c - 1:c, :],
                a_ref[ci * c:(ci + 1) * c, w3:w3 + DN_WIDTH]) for ci in chunks]

    s = s_scr[...]
    for ci, (u, w, qkm, qdec, kdec, glrow, gate) in enumerate(prepped):
        s16 = s.astype(BF16)
        vnew = u - _dot(w.astype(BF16), s16)
        v16 = vnew.astype(BF16)
        o = _dot(qdec.astype(BF16), s16) + _dot(qkm.astype(BF16), bd(v16))
        s = glrow * s + jnp.where(bdmask, _dot_tn(kdec.astype(BF16), v16), 0.0)
        ms = _dot_r01(o * o, bd1, 2) * (1.0 / DN_HEAD_DIM)
        o = o * lax.rsqrt(ms + EPS) * dnn_ref[...] * _silu(gate)
        o_ref[ci * c:(ci + 1) * c, :] = o.astype(BF16)
    s_scr[...] = s
    sout_ref[...] = s


def _delta(a, ab, cw, alog, dtb, dnn, e2, lmat, bd1, conv0, s0, bsz, t_pad, t_valid, tb):
    nblk = t_pad // tb
    nv_last = t_valid - (nblk - 1) * tb
    full = lambda x: pl.BlockSpec(x.shape, lambda b, j: (0,) * x.ndim)
    per_b = lambda x: pl.BlockSpec((None,) + x.shape[1:], lambda b, j: (b,) + (0,) * (x.ndim - 1))
    return pl.pallas_call(
        functools.partial(_delta_kernel, tb=tb, nv_last=nv_last, mask_rows=(t_valid != t_pad)),
        grid=(bsz, nblk),
        in_specs=[pl.BlockSpec((tb, A_COLS), lambda b, j: (b * nblk + j, 0)),
                  pl.BlockSpec((tb, LANES), lambda b, j: (b * nblk + j, 0)),
                  full(cw), full(alog), full(dtb), full(dnn), full(e2), full(lmat), full(bd1),
                  per_b(conv0), per_b(s0)],
        out_specs=(pl.BlockSpec((tb, DN_WIDTH), lambda b, j: (b * nblk + j, 0)), per_b(conv0), per_b(s0)),
        out_shape=(jax.ShapeDtypeStruct((bsz * t_pad, DN_WIDTH), BF16),
                   jax.ShapeDtypeStruct(conv0.shape, F32),
                   jax.ShapeDtypeStruct(s0.shape, F32)),
        scratch_shapes=[pltpu.VMEM((CONV_PAD + tb, 3 * DN_WIDTH), F32), pltpu.VMEM((DN_PACK, DN_PACK), F32)],
        compiler_params=_cparams(("parallel", "arbitrary")),
        name="gated_delta",
    )(a, ab, cw, alog, dtb, dnn, e2, lmat, bd1, conv0, s0)


def _delta_step_kernel(a_ref, ab_ref, cw_ref, alog_ref, dtb_ref, dnn_ref, e2_ref, bd1_ref, conv_ref, s_ref,
                       o_ref, convn_ref, sout_ref):
    nseq = a_ref.shape[0]
    w3 = 3 * DN_WIDTH
    x = a_ref[:, 0:w3]
    conv = cw_ref[DN_CONV - 1:DN_CONV, :] * x
    for i in range(DN_CONV - 1):
        conv = conv + cw_ref[i:i + 1, :] * conv_ref[i]
        if i > 0:
            convn_ref[i - 1] = conv_ref[i]
    convn_ref[DN_CONV - 2] = x
    r = _silu(conv)
    bd1 = bd1_ref[...]
    q = r[:, 0:DN_WIDTH]
    k = r[:, DN_WIDTH:2 * DN_WIDTH]
    v = r[:, 2 * DN_WIDTH:w3]
    q = q * lax.rsqrt(_dot_r01(q * q, bd1, 2) + EPS) * (DN_HEAD_DIM ** -0.5)
    k = k * lax.rsqrt(_dot_r01(k * k, bd1, 2) + EPS)
    abv = ab_ref[...]
    lane8 = lax.broadcasted_iota(jnp.int32, abv.shape, 1)
    g_all = -jnp.exp(alog_ref[...]) * _softplus(abv + dtb_ref[...])
    gbx = _dot_r01(jnp.where(lane8 < DN_HEADS, g_all, _sigmoid(abv)), e2_ref[...], 3)
    eg = jnp.exp(gbx[:, 0:DN_PACK])
    bexp = gbx[:, DN_PACK:]
    qk = _dot_r01(q * k, bd1, 2)
    wk = bexp * eg * k
    qd = q * eg
    row = lax.broadcasted_iota(jnp.int32, (nseq, DN_PACK), 0)
    bdmask = (lax.broadcasted_iota(jnp.int32, (DN_PACK, DN_PACK), 0) // DN_HEAD_DIM
              == lax.broadcasted_iota(jnp.int32, (DN_PACK, DN_PACK), 1) // DN_HEAD_DIM)

    def only(s, x):
        return jnp.where(row == s, x, 0.0).astype(BF16)

    s16 = [s_ref[s].astype(BF16) for s in range(nseq)]
    ws = sum(_dot(only(s, wk), s16[s]) for s in range(nseq))
    qs = sum(_dot(only(s, qd), s16[s]) for s in range(nseq))
    vnew = bexp * v - ws
    o = qs + qk * vnew
    ms = _dot_r01(o * o, bd1, 2) * (1.0 / DN_HEAD_DIM)
    o = o * lax.rsqrt(ms + EPS) * dnn_ref[...] * _silu(a_ref[:, w3:w3 + DN_WIDTH])
    o_ref[...] = o.astype(BF16)
    v16 = vnew.astype(BF16)
    for s in range(nseq):
        outer = _dot_tn(only(s, k), v16)
        sout_ref[s] = eg[s:s + 1, :] * s_ref[s] + jnp.where(bdmask, outer, 0.0)


def _delta_step(a, ab, cw, alog, dtb, dnn, e2, bd1, conv_t, s0):
    nseq = a.shape[0]
    blk = SUBLANES
    full = lambda x: pl.BlockSpec(x.shape, lambda i: (0,) * x.ndim)
    conv_spec = pl.BlockSpec((DN_CONV - 1, blk, 3 * DN_WIDTH), lambda i: (0, i, 0))
    s_spec = pl.BlockSpec((blk, DN_PACK, DN_PACK), lambda i: (i, 0, 0))
    return pl.pallas_call(
        _delta_step_kernel,
        grid=(nseq // blk,),
        in_specs=[pl.BlockSpec((blk, A_COLS), lambda i: (i, 0)), pl.BlockSpec((blk, LANES), lambda i: (i, 0)),
                  full(cw), full(alog), full(dtb), full(dnn), full(e2), full(bd1), conv_spec, s_spec],
        out_specs=(pl.BlockSpec((blk, DN_WIDTH), lambda i: (i, 0)), conv_spec, s_spec),
        out_shape=(jax.ShapeDtypeStruct((nseq, DN_WIDTH), BF16),
                   jax.ShapeDtypeStruct(conv_t.shape, F32),
                   jax.ShapeDtypeStruct(s0.shape, F32)),
        compiler_params=_cparams(("parallel",)),
        name="gated_delta_step",
    )(a, ab, cw, alog, dtb, dnn, e2, bd1, conv_t, s0)


def _delta_constants():
    c = DN_CHUNK
    lane = jnp.arange(2 * DN_PACK)
    row = jnp.arange(LANES)[:, None]
    e2 = (((lane[None, :] < DN_PACK) & (row == lane[None, :] // c))
          | ((lane[None, :] >= DN_PACK) & (row == DN_HEADS + (lane[None, :] - DN_PACK) // c))).astype(BF16)
    i = jnp.arange(c)[:, None]
    m = jnp.arange(c)[None, :]
    lmat = jnp.concatenate([(m <= i), (m > i)], axis=0).astype(BF16)
    hh = jnp.arange(DN_PACK) // c
    bd1 = (hh[:, None] == hh[None, :]).astype(BF16)
    return e2, lmat, bd1


def _tri_incl(n):
    return (jnp.arange(n)[:, None] >= jnp.arange(n)[None, :]).astype(BF16)


def _seg_matrices():
    c = jnp.arange(2 * SB_WIDTH)[:, None] // SB_HEAD_DIM
    j = jnp.arange(LANES)[None, :]
    seg = (c == j).astype(BF16)
    return seg, seg.T


def _prep_w_in(w):
    n_ab = 2 * DN_HEADS
    c1 = A_COLS
    c2 = c1 + n_ab
    c3 = c2 + B_COLS
    pad = jnp.zeros((D_MODEL, LANES - n_ab), w.dtype)
    return jnp.concatenate([w[:, :c1], w[:, c2:c3], w[:, c3:], w[:, c1:c2], pad], axis=1).astype(BF16)


def _row(v, width):
    return jnp.pad(v.astype(F32), (0, width - v.shape[0]))[None]


def _pack_state(s):
    eye = jnp.eye(DN_HEADS, dtype=s.dtype)
    return jnp.einsum('bhkv,hg->bhkgv', s, eye).reshape(s.shape[0], DN_PACK, DN_PACK)


def _unpack_state(sp):
    b = sp.shape[0]
    sp = sp.reshape(b, DN_HEADS, DN_HEAD_DIM, DN_HEADS, DN_HEAD_DIM)
    return jnp.stack([sp[:, h, :, h, :] for h in range(DN_HEADS)], axis=1)


TM_PROJ = 256
TM_FFN = 512
TH_FFN = 6 * MXU_TILE
TB_DELTA = 4 * DN_CHUNK
TQ_ATTN = 512
S5_ROWS = 1024


def _mixers(x, lw, consts, bsz, t, conv0, s0, x0_ssm, attn_fn, layer, kv_bufs):
    (w16, qkn, cw, alog, dtb, dnn, wb, lam, cm, dskip, gw16, gb, wo16, ffn_g, wg16, wu16, wd16, attn_g) = lw
    seg, segt, e2, lmat, bd1 = consts
    n = bsz * t
    tm = min(TM_PROJ, n)
    a, ab, q16, k16, v16, kf, vf, u, u_time_major = _proj_in(x, attn_g, w16, qkn, seg, segt, tm, bsz, t,
                                                             layer, kv_bufs)

    hist = CONV_PAD - (DN_CONV - 1)
    if t == 1 and bsz % SUBLANES == 0:
        o_a, conv_t, s_new = _delta_step(a, ab, cw, alog, dtb, dnn, e2, bd1,
                                         conv0[:, hist:].transpose(1, 0, 2), s0)
        convn = jnp.pad(conv_t.transpose(1, 0, 2), ((0, 0), (hist, 0), (0, 0)))
    else:
        tb = min(TB_DELTA, -(-t // DN_CHUNK) * DN_CHUNK)
        t_pad = -(-t // tb) * tb
        if t_pad != t:
            padt = lambda y: jnp.pad(y.reshape(bsz, t, -1), ((0, 0), (0, t_pad - t), (0, 0))).reshape(bsz * t_pad, -1)
            a_in, ab_in = padt(a), padt(ab)
        else:
            a_in, ab_in = a, ab
        o_a, convn, s_new = _delta(a_in, ab_in, cw, alog, dtb, dnn, e2, lmat, bd1, conv0, s0, bsz, t_pad, t, tb)
        if t_pad != t:
            o_a = o_a.reshape(bsz, t_pad, DN_WIDTH)[:, :t].reshape(n, DN_WIDTH)

    o_b = attn_fn(q16, k16, v16)

    nb = bsz if (bsz * 2 == SUBLANES and t % 2 == 0) else -(-bsz // SUBLANES) * SUBLANES
    steps = min(S5_ROWS // nb, t)
    tm_ffn = min(TM_FFN, n)
    u_tb = u.reshape(t, bsz, SSM_WIDTH) if u_time_major else u.reshape(bsz, t, SSM_WIDTH).transpose(1, 0, 2)
    if nb != bsz:
        u_tb = jnp.pad(u_tb, ((0, 0), (0, nb - bsz), (0, 0)))
    o_c, xt = _s5(u_tb.reshape(t * nb, SSM_WIDTH), wb, lam, cm, dskip, gw16, gb,
                  jnp.pad(x0_ssm, ((0, nb - bsz), (0, 0))), steps, nb)
    direct = u_time_major and nb == bsz and t % tm_ffn == 0
    if not direct:
        o_c = o_c.reshape(t, nb, SSM_WIDTH)[:, :bsz].transpose(1, 0, 2).reshape(n, SSM_WIDTH)

    x = _out_ffn(x, o_a, o_b, o_c, wo16, ffn_g, wg16, wu16, wd16, tm_ffn, TH_FFN, t if direct else None)
    leaves = (convn[:, CONV_PAD - (DN_CONV - 1):], _unpack_state(s_new),
              xt[:bsz, :SSM_FLAT].reshape(bsz, SSM_GROUPS, SSM_STATE),
              xt[:bsz, SSM_FLAT:].reshape(bsz, SSM_GROUPS, SSM_STATE))
    return x, (kf, vf), leaves


def kernel(x_prompt, x_sample, cache_k, cache_v, page_table, state_conv, state_delta, state_ssm_re, state_ssm_im,
           attn_norm, w_in, conv_w, dn_a_log, dn_dt_bias, dn_out_norm, sb_q_norm, sb_k_norm, sb_bias,
           ssm_lambda_re, ssm_lambda_im, ssm_log_step, ssm_b_re, ssm_b_im, ssm_c_re, ssm_c_im, ssm_d,
           ssm_glu_w, ssm_glu_b, w_out, ffn_norm, w_gate, w_up, w_down):
    bp, tp, _ = x_prompt.shape
    bs, ts, _ = x_sample.shape
    assert ts == 1, "the sample group decodes one token per sequence"
    n_pool = cache_k.shape[1]
    ck = cache_k.transpose(0, 1, 3, 4, 2).reshape(DEPTH, n_pool, SB_WIDTH, PAGE_SIZE)
    cv = cache_v.transpose(0, 1, 3, 4, 2).reshape(DEPTH, n_pool, SB_WIDTH, PAGE_SIZE)
    consts = _seg_matrices() + _delta_constants()
    tri_p = _tri_incl(TQ_ATTN)
    tri_d = _tri_incl(PAGE_SIZE)

    xp = x_prompt.reshape(bp * tp, D_MODEL)
    xs = x_sample.reshape(bs * ts, D_MODEL)
    conv0_p = jnp.zeros((bp, CONV_PAD, 3 * DN_WIDTH), F32)
    s0_p = jnp.zeros((bp, DN_PACK, DN_PACK), F32)
    x0_p = jnp.zeros((bp, 2 * SSM_FLAT), F32)
    acc = [[] for _ in range(8)]
    kv_p = kv_s = None
    for l in range(DEPTH):
        wb, cm, lam = _prep_s5(ssm_lambda_re[l], ssm_lambda_im[l], ssm_log_step[l],
                               ssm_b_re[l], ssm_b_im[l], ssm_c_re[l], ssm_c_im[l])
        qkn = jnp.concatenate([jnp.tile(sb_q_norm[l], SB_HEADS), jnp.tile(sb_k_norm[l], SB_HEADS)])[None]
        lw = (_prep_w_in(w_in[l]), qkn, conv_w[l], _row(dn_a_log[l], LANES), _row(dn_dt_bias[l], LANES),
              jnp.tile(dn_out_norm[l], DN_HEADS)[None], wb, lam, cm, ssm_d[l][None],
              ssm_glu_w[l].astype(BF16), ssm_glu_b[l][None], w_out[l].astype(BF16), ffn_norm[l][None],
              w_gate[l].astype(BF16), w_up[l].astype(BF16), w_down[l].astype(BF16), attn_norm[l][None])
        bias = sb_bias[l].astype(F32)

        attn_p = lambda q, k, v: _attn_prompt(q, k, v, bias, tri_p, bp, tp, TQ_ATTN)
        xp, kv_p, st_p = _mixers(xp, lw, consts, bp, tp, conv0_p, s0_p, x0_p, attn_p, l, kv_p)

        bias_col = jnp.broadcast_to(bias[:, None], (SB_HEADS, LANES))
        attn_s = lambda q, k, v: _attn_decode(q.reshape(bs, 1, SB_WIDTH), bias_col, tri_d, page_table,
                                              ck, cv, l).reshape(bs, SB_WIDTH)
        conv0_s = jnp.pad(state_conv[l], ((0, 0), (CONV_PAD - (DN_CONV - 1), 0), (0, 0)))
        x0_s = jnp.concatenate([state_ssm_re[l].reshape(bs, SSM_FLAT), state_ssm_im[l].reshape(bs, SSM_FLAT)], axis=1)
        xs, kv_s, st_s = _mixers(xs, lw, consts, bs, ts, conv0_s, _pack_state(state_delta[l]), x0_s, attn_s, l, kv_s)
        for a, val in zip(acc, st_p + st_s):
            a.append(val)
    st = [jnp.stack(a, axis=0) for a in acc]
    def heads(buf, b, t):
        if buf.ndim == 4:
            return buf.reshape(DEPTH, b, SB_HEADS, SB_HEAD_DIM, t).transpose(0, 1, 4, 2, 3)
        return buf.reshape(DEPTH, b, t, SB_HEADS, SB_HEAD_DIM)

    return (xp.reshape(bp, tp, D_MODEL), xs.reshape(bs, ts, D_MODEL),
            heads(kv_p[0], bp, tp), heads(kv_p[1], bp, tp), *st[:4],
            heads(kv_s[0], bs, ts), heads(kv_s[1], bs, ts), *st[4:])
```

```python
import functools
import math

import jax
import jax.numpy as jnp
from jax import lax
from jax.experimental import pallas as pl
from jax.experimental.pallas import tpu as pltpu

F32 = jnp.float32
BF16 = jnp.bfloat16

D_MODEL = 1024
DEPTH = 4
PAGE_SIZE = 128
DN_HEADS = 4
DN_HEAD_DIM = 64
DN_WIDTH = DN_HEADS * DN_HEAD_DIM
DN_CONV = 4
DN_CHUNK = 64
SB_HEADS = 8
SB_HEAD_DIM = 64
SB_WIDTH = SB_HEADS * SB_HEAD_DIM
SSM_WIDTH = D_MODEL - DN_WIDTH - SB_WIDTH
SSM_GROUP = 16
SSM_GROUPS = SSM_WIDTH // SSM_GROUP
SSM_STATE = 64
SSM_FLAT = SSM_GROUPS * SSM_STATE
FFN_HIDDEN = 2816
EPS = 1e-6

LANES = 128
SUBLANES = 8
VMEM_LIMIT = 56 * 1024 * 1024

A_COLS = 4 * DN_WIDTH
B_COLS = 3 * SB_WIDTH
IN_PAD = A_COLS + B_COLS + SSM_WIDTH + LANES


def _cparams(sem):
    return pltpu.CompilerParams(dimension_semantics=sem, vmem_limit_bytes=VMEM_LIMIT)


def _split(x, n):
    terms = []
    r = x
    for i in range(n):
        t = r.astype(BF16)
        terms.append(t)
        if i + 1 < n:
            r = r - t.astype(F32)
    return terms


def _dot(a, b):
    return jnp.dot(a, b, preferred_element_type=F32)


def _dot_l01(a01, x, n):
    return sum(_dot(a01, t) for t in _split(x, n))


def _dot_r01(x, b01, n):
    return sum(_dot(t, b01) for t in _split(x, n))


def _sigmoid(x):
    return 1.0 / (1.0 + jnp.exp(-x))


def _silu(x):
    return x * _sigmoid(x)


LOG2E = 1.4426950408889634


def _softplus(x):
    return jnp.maximum(x, 0.0) + jnp.log(1.0 + jnp.exp2(jnp.abs(x) * (-LOG2E)))


def _proj_in_kernel(x_ref, g_ref, w_ref, qkn_ref, seg_ref, segt_ref, *refs, feature_major):
    a_ref, ab_ref, q16_ref, k16_ref, v16_ref, kf_ref, vf_ref, u_ref = refs[-8:]
    x = x_ref[...]
    ms = jnp.mean(x * x, axis=-1, keepdims=True)
    h = (x * lax.rsqrt(ms + EPS) * g_ref[...]).astype(BF16)
    a_ref[...] = _dot(h, w_ref[:, 0:A_COLS])
    c0 = A_COLS
    qk = _dot(h, w_ref[:, c0:c0 + 2 * SB_WIDTH])
    v = _dot(h, w_ref[:, c0 + 2 * SB_WIDTH:c0 + B_COLS])
    c0 += B_COLS
    u_ref[...] = _dot(h, w_ref[:, c0:c0 + SSM_WIDTH])
    c0 += SSM_WIDTH
    ab_ref[...] = _dot(h, w_ref[:, c0:c0 + LANES])
    ssq = _dot_r01(qk * qk, seg_ref[...], 1)
    inv = lax.rsqrt(ssq * (1.0 / SB_HEAD_DIM) + EPS)
    qkn = qk * _dot_r01(inv, segt_ref[...], 2) * qkn_ref[...]
    qn = qkn[:, 0:SB_WIDTH]
    kn = qkn[:, SB_WIDTH:]
    q16_ref[...] = (qn * (SB_HEAD_DIM ** -0.5)).astype(BF16)
    k16_ref[...] = kn.astype(BF16)
    v16_ref[...] = v.astype(BF16)
    if feature_major:
        kf_ref[...] = kn.T
        vf_ref[...] = v.T
    else:
        kf_ref[...] = kn
        vf_ref[...] = v


def _proj_in(x, gain, w16, qkn, seg, segt, tm, bsz, t, layer, kv_bufs):
    n = x.shape[0]
    row = lambda w: pl.BlockSpec((tm, w), lambda i: (i, 0))
    full = lambda a: pl.BlockSpec(a.shape, lambda i: (0,) * a.ndim)
    feature_major = t % tm == 0 and tm % LANES == 0
    if feature_major:
        nt = t // tm
        slab = pl.BlockSpec((None, None, SB_WIDTH, tm), lambda i: (layer, i // nt, 0, i % nt))
        kv_shape = jax.ShapeDtypeStruct((DEPTH, bsz, SB_WIDTH, t), F32)
        u_spec = pl.BlockSpec((tm, SSM_WIDTH), lambda i: (i % nt, i // nt))
        u_shape = jax.ShapeDtypeStruct((t, bsz * SSM_WIDTH), F32)
    else:
        slab = pl.BlockSpec((None, tm, SB_WIDTH), lambda i: (layer, i, 0))
        kv_shape = jax.ShapeDtypeStruct((DEPTH, n, SB_WIDTH), F32)
        u_spec = row(SSM_WIDTH)
        u_shape = jax.ShapeDtypeStruct((n, SSM_WIDTH), F32)
    out_shapes = (
        jax.ShapeDtypeStruct((n, A_COLS), F32),
        jax.ShapeDtypeStruct((n, LANES), F32),
        jax.ShapeDtypeStruct((n, SB_WIDTH), BF16),
        jax.ShapeDtypeStruct((n, SB_WIDTH), BF16),
        jax.ShapeDtypeStruct((n, SB_WIDTH), BF16),
        kv_shape,
        kv_shape,
        u_shape,
    )
    if kv_bufs is None:
        kv_bufs = (jnp.zeros(kv_shape.shape, F32), jnp.zeros(kv_shape.shape, F32))
    in_specs = ([row(D_MODEL), full(gain), full(w16), full(qkn), full(seg), full(segt)]
                + [pl.BlockSpec(memory_space=pl.ANY)] * 2)
    args = [x, gain, w16, qkn, seg, segt, *kv_bufs]
    aliases = {len(args) - 2: 5, len(args) - 1: 6}
    outs = pl.pallas_call(
        functools.partial(_proj_in_kernel, feature_major=feature_major),
        grid=(n // tm,),
        in_specs=in_specs,
        out_specs=(row(A_COLS), row(LANES), row(SB_WIDTH), row(SB_WIDTH), row(SB_WIDTH),
                   slab, slab, u_spec),
        out_shape=out_shapes,
        input_output_aliases=aliases,
        compiler_params=_cparams(("parallel",)),
        name="proj_in",
    )(*args)
    return outs[:7] + (outs[7].reshape(n, SSM_WIDTH), feature_major)


MXU_TILE = 256


def _hidden_splits(th_max):
    assert FFN_HIDDEN % MXU_TILE == 0
    tiles = FFN_HIDDEN // MXU_TILE
    per = max(th_max // MXU_TILE, 1)
    cuts = list(range(0, tiles, per)) + [tiles]
    return [(lo * MXU_TILE, hi * MXU_TILE) for lo, hi in zip(cuts[:-1], cuts[1:])]


def _out_ffn_kernel(x_ref, oa_ref, ob_ref, oc_ref, wo_ref, g_ref, wg_ref, wu_ref, wd_ref, y_ref, *, th):
    mix = (_dot(oa_ref[...], wo_ref[0:DN_WIDTH, :])
           + _dot(ob_ref[...], wo_ref[DN_WIDTH:DN_WIDTH + SB_WIDTH, :])
           + _dot(oc_ref[...], wo_ref[DN_WIDTH + SB_WIDTH:, :]))
    x1 = x_ref[...] + mix
    ms = jnp.mean(x1 * x1, axis=-1, keepdims=True)
    h = (x1 * lax.rsqrt(ms + EPS) * g_ref[...]).astype(BF16)
    y = x1
    for lo, hi in _hidden_splits(th):
        gate = _dot(h, wg_ref[:, lo:hi])
        up = _dot(h, wu_ref[:, lo:hi])
        y = y + _dot((_silu(gate) * up).astype(BF16), wd_ref[lo:hi, :])
    y_ref[...] = y


def _out_ffn(x, oa, ob, oc, wo16, gain, wg16, wu16, wd16, tm, th, oc_time_major_t=None):
    n = x.shape[0]
    row = lambda w: pl.BlockSpec((tm, w), lambda i: (i, 0))
    fixed = lambda a: pl.BlockSpec(a.shape, lambda i: (0,) * a.ndim, pipeline_mode=pl.Buffered(1))
    oc_spec = row(SSM_WIDTH)
    if oc_time_major_t is not None:
        t = oc_time_major_t
        nt = t // tm
        oc = oc.reshape(t, (n // t) * SSM_WIDTH)
        oc_spec = pl.BlockSpec((tm, SSM_WIDTH), lambda i: (i % nt, i // nt))
    return pl.pallas_call(
        functools.partial(_out_ffn_kernel, th=th),
        grid=(n // tm,),
        in_specs=[row(D_MODEL), row(DN_WIDTH), row(SB_WIDTH), oc_spec,
                  fixed(wo16), fixed(gain), fixed(wg16), fixed(wu16), fixed(wd16)],
        out_specs=row(D_MODEL),
        out_shape=jax.ShapeDtypeStruct((n, D_MODEL), F32),
        compiler_params=_cparams(("parallel",)),
        name="out_ffn",
    )(x, oa, ob, oc, wo16, gain, wg16, wu16, wd16)


def _gelu_tanh(x):
    c = math.sqrt(2.0 / math.pi)
    return 0.5 * x * (1.0 + jnp.tanh(c * (x + 0.044715 * (x * x * x))))


def _s5_kernel(u_ref, wb_ref, lam_ref, cm_ref, d_ref, gw_ref, gb_ref, x0_ref,
               o_ref, xt_ref, bu_scr, x_scr, *, steps, nb):
    two_step = nb * 2 == SUBLANES
    tile = SUBLANES if two_step else nb

    @pl.when(pl.program_id(0) == 0)
    def _():
        x_scr[...] = jnp.concatenate([x0_ref[...]] * 2, axis=0) if two_step else x0_ref[...]

    u = u_ref[...]
    bu_scr[...] = _dot(u.astype(BF16), wb_ref[...])
    lre = jnp.broadcast_to(lam_ref[:, 0:SSM_FLAT], (tile, SSM_FLAT))
    lim = jnp.broadcast_to(lam_ref[:, SSM_FLAT:], (tile, SSM_FLAT))
    lower = lax.broadcasted_iota(jnp.int32, (tile, 2 * SSM_FLAT), 0) < nb

    def advance(x, bu):
        xr = x[:, 0:SSM_FLAT]
        xi = x[:, SSM_FLAT:]
        nr = lre * xr - lim * xi + bu[:, 0:SSM_FLAT]
        ni = lre * xi + lim * xr + bu[:, SSM_FLAT:]
        return jnp.concatenate([nr, ni], axis=1)

    def step(t, x):
        r0 = pl.multiple_of(t * tile, tile)
        bu = bu_scr[pl.ds(r0, tile), :]
        x1 = advance(x, bu)
        if two_step:
            x2 = advance(pltpu.roll(x1, nb, 0), bu)
            bu_scr[pl.ds(r0, tile), :] = jnp.where(lower, x1, x2)
            return jnp.where(lower, pltpu.roll(x2, nb, 0), x2)
        bu_scr[pl.ds(r0, tile), :] = x1
        return x1

    x = lax.fori_loop(0, steps * nb // tile, step, x_scr[...])
    x_scr[...] = x
    xt_ref[...] = x[0:nb, :]
    y = _dot(bu_scr[...].astype(BF16), cm_ref[...]) + d_ref[...] * u
    z = _gelu_tanh(y)
    o_ref[...] = (z * _sigmoid(_dot(z.astype(BF16), gw_ref[...]) + gb_ref[...])).astype(BF16)


def _s5(u_rows, wb, lam, cm, d, gw16, gb, x0, steps, nb):
    assert nb % SUBLANES == 0 or (nb * 2 == SUBLANES and steps % 2 == 0)
    n = u_rows.shape[0]
    rows = steps * nb
    full = lambda a: pl.BlockSpec(a.shape, lambda i: (0,) * a.ndim)
    return pl.pallas_call(
        functools.partial(_s5_kernel, steps=steps, nb=nb),
        grid=(n // rows,),
        in_specs=[pl.BlockSpec((rows, SSM_WIDTH), lambda i: (i, 0)),
                  full(wb), full(lam), full(cm), full(d), full(gw16), full(gb), full(x0)],
        out_specs=(pl.BlockSpec((rows, SSM_WIDTH), lambda i: (i, 0)), full(x0)),
        out_shape=(jax.ShapeDtypeStruct((n, SSM_WIDTH), BF16),
                   jax.ShapeDtypeStruct(x0.shape, F32)),
        scratch_shapes=[pltpu.VMEM((rows, 2 * SSM_FLAT), F32),
                        pltpu.VMEM((max(nb, SUBLANES), 2 * SSM_FLAT), F32)],
        compiler_params=_cparams(("arbitrary",)),
        name="s5_scan",
    )(u_rows, wb, lam, cm, d, gw16, gb, x0)


def _prep_s5(lam_re, lam_im, log_step, b_re, b_im, c_re, c_im):
    step = jnp.exp(log_step)[:, None]
    mag = jnp.exp(lam_re * step)
    ang = lam_im * step
    lb_re, lb_im = mag * jnp.cos(ang), mag * jnp.sin(ang)
    den = lam_re * lam_re + lam_im * lam_im
    f_re = ((lb_re - 1.0) * lam_re + lb_im * lam_im) / den
    f_im = (lb_im * lam_re - (lb_re - 1.0) * lam_im) / den
    bb_re = f_re[..., None] * b_re - f_im[..., None] * b_im
    bb_im = f_re[..., None] * b_im + f_im[..., None] * b_re
    eye = jnp.eye(SSM_GROUPS, dtype=F32)
    pack_b = lambda bb: jnp.einsum('gpc,gh->gchp', bb, eye).reshape(SSM_WIDTH, SSM_FLAT)
    pack_c = lambda cc: jnp.einsum('gcp,gh->gphc', cc, eye).reshape(SSM_FLAT, SSM_WIDTH)
    wb = jnp.concatenate([pack_b(bb_re), pack_b(bb_im)], axis=1).astype(BF16)
    cm = jnp.concatenate([pack_c(c_re), -pack_c(c_im)], axis=0).astype(BF16)
    lam = jnp.concatenate([lb_re.reshape(1, SSM_FLAT), lb_im.reshape(1, SSM_FLAT)], axis=1)
    return wb, cm, lam


HEAD_GROUP = 4
GROUP_W = HEAD_GROUP * SB_HEAD_DIM


def _dot_nt(a, b):
    return lax.dot_general(a, b, (((1,), (1,)), ((), ())), preferred_element_type=F32)


def _attn_kernel(bias_ref, q_ref, k_ref, v_ref, tri_ref, o_ref, acc_scr, z_scr, w_scr, vs_scr, *, tq):
    g = pl.program_id(1)
    i = pl.program_id(2)
    q = q_ref[...]
    lane_head = lax.broadcasted_iota(jnp.int32, (tq, GROUP_W), 1) // SB_HEAD_DIM

    @pl.when(i == 0)
    def _():
        def fill(j, _):
            vb = v_ref[pl.ds(pl.multiple_of(j * tq, tq), tq), :]
            for hh in range(HEAD_GROUP):
                vs_scr[j, hh * tq:(hh + 1) * tq, :] = jnp.where(lane_head == hh, vb, jnp.zeros_like(vb))
            return 0
        lax.fori_loop(0, vs_scr.shape[0], fill, 0)
    causal = (lax.broadcasted_iota(jnp.int32, (tq, tq), 1)
              < lax.broadcasted_iota(jnp.int32, (tq, tq), 0))
    tri = tri_ref[...]
    heads = range(HEAD_GROUP)
    qms = [jnp.where(lane_head == hh, q, jnp.zeros_like(q)) for hh in heads]
    biases = [bias_ref[g * HEAD_GROUP + hh] for hh in heads]

    r0 = pl.multiple_of(i * tq, tq)

    def suffix_sum(sp):
        return _dot(sp.astype(BF16), tri)

    def scores(c0):
        kb = k_ref[pl.ds(c0, tq), :]
        return [_dot_nt(qms[hh], kb) + biases[hh] for hh in heads]

    zs = scores(r0)
    cs = [suffix_sum(jnp.where(causal, _softplus(zs[hh]), 0.0)) for hh in heads]
    for hh in heads:
        w_scr[:, hh * tq:(hh + 1) * tq] = jnp.where(causal, jnp.exp(zs[hh] - cs[hh]), 0.0).astype(BF16)
    acc_scr[...] = jnp.zeros((tq, GROUP_W), F32)
    zn = scores(pl.multiple_of(jnp.maximum(i - 1, 0) * tq, tq))
    for hh in heads:
        z_scr[hh] = zn[hh]
    carries = tuple(cb[:, 0:1] for cb in cs)

    def body(jj, carries):
        zc = [z_scr[hh] for hh in heads]
        zn = scores(pl.multiple_of(jnp.maximum(i - jj - 1, 0) * tq, tq))
        acc_scr[...] += _dot(w_scr[...], vs_scr[i - jj + 1])
        cs = [suffix_sum(_softplus(zc[hh])) + carries[hh] for hh in heads]
        for hh in heads:
            w_scr[:, hh * tq:(hh + 1) * tq] = jnp.exp(zc[hh] - cs[hh]).astype(BF16)
            z_scr[hh] = zn[hh]
        return tuple(cb[:, 0:1] for cb in cs)

    lax.fori_loop(1, i + 1, body, carries)
    o_ref[...] = (acc_scr[...] + _dot(w_scr[...], vs_scr[0])).astype(BF16)


def _attn_prompt(q16, k16, v16, bias, tri, bsz, t, tq):
    nq = t // tq
    ngrp = SB_WIDTH // GROUP_W
    grid_spec = pltpu.PrefetchScalarGridSpec(
        num_scalar_prefetch=1,
        grid=(bsz, ngrp, nq),
        in_specs=[pl.BlockSpec((tq, GROUP_W), lambda b, g, i, s: (b * nq + i, g)),
                  pl.BlockSpec((t, GROUP_W), lambda b, g, i, s: (b, g)),
                  pl.BlockSpec((t, GROUP_W), lambda b, g, i, s: (b, g)),
                  pl.BlockSpec((tq, tq), lambda b, g, i, s: (0, 0))],
        out_specs=pl.BlockSpec((tq, GROUP_W), lambda b, g, i, s: (b * nq + i, g)),
        scratch_shapes=[pltpu.VMEM((tq, GROUP_W), F32),
                        pltpu.VMEM((HEAD_GROUP, tq, tq), F32),
                        pltpu.VMEM((tq, HEAD_GROUP * tq), BF16),
                        pltpu.VMEM((nq, HEAD_GROUP * tq, GROUP_W), BF16)],
    )
    return pl.pallas_call(
        functools.partial(_attn_kernel, tq=tq),
        grid_spec=grid_spec,
        out_shape=jax.ShapeDtypeStruct((bsz * t, SB_WIDTH), BF16),
        compiler_params=_cparams(("arbitrary", "arbitrary", "arbitrary")),
        name="sb_attn_prompt",
    )(bias, q16, k16, v16, tri)


PAGES_PER_STEP = 16


def _attn_decode_kernel(pt_ref, q_ref, bias_ref, tri_ref, *refs, pp):
    del pt_ref
    k_refs = refs[:pp]
    v_refs = refs[pp:2 * pp]
    o_ref, acc_scr, carry_scr = refs[2 * pp:]
    j = pl.program_id(1)

    @pl.when(j == 0)
    def _():
        acc_scr[...] = jnp.zeros_like(acc_scr)
        carry_scr[...] = jnp.zeros_like(carry_scr)

    on_diag = (lax.broadcasted_iota(jnp.int32, (SB_HEADS, SB_WIDTH), 0)
               == lax.broadcasted_iota(jnp.int32, (SB_HEADS, SB_WIDTH), 1) // SB_HEAD_DIM)
    q = jnp.broadcast_to(q_ref[0].astype(F32), (SB_HEADS, SB_WIDTH))
    qbd = jnp.where(on_diag, q, 0.0).astype(BF16)
    bias = bias_ref[...]
    tri = tri_ref[...]
    kcat = jnp.concatenate([k_refs[p][...].astype(BF16) for p in range(pp)], axis=1)
    zw = _dot(qbd, kcat)
    z = jnp.concatenate([zw[:, p * PAGE_SIZE:(p + 1) * PAGE_SIZE] + bias for p in range(pp)], axis=0)
    cum = _dot_r01(_softplus(z), tri, 2)
    carry = carry_scr[:, 0:1]
    offs = []
    for p in range(pp):
        offs.append(carry)
        carry = carry + cum[p * SB_HEADS:(p + 1) * SB_HEADS, 0:1]
    w = jnp.exp(z - cum - jnp.concatenate(offs, axis=0)).astype(BF16)
    wcat = jnp.concatenate([w[p * SB_HEADS:(p + 1) * SB_HEADS, :] for p in range(pp)], axis=1)
    vcat = jnp.concatenate([v_refs[p][...].astype(BF16) for p in range(pp)], axis=1)
    acc = acc_scr[...] + _dot_nt(wcat, vcat)
    acc_scr[...] = acc
    carry_scr[...] = jnp.broadcast_to(carry, carry_scr.shape)

    @pl.when(j == pl.num_programs(1) - 1)
    def _():
        o_ref[0] = jnp.sum(jnp.where(on_diag, acc, 0.0), axis=0, keepdims=True).astype(BF16)


def _attn_decode(q16, bias_col, tri, page_table, cache_k, cache_v, layer):
    nseq, npg = page_table.shape
    pp = PAGES_PER_STEP

    def page_spec(p):
        return pl.BlockSpec((None, None, SB_WIDTH, PAGE_SIZE),
                            lambda b, j, pt: (layer, pt[b, npg - 1 - (j * pp + p)], 0, 0))

    grid_spec = pltpu.PrefetchScalarGridSpec(
        num_scalar_prefetch=1,
        grid=(nseq, npg // pp),
        in_specs=[pl.BlockSpec((1, 1, SB_WIDTH), lambda b, j, pt: (b, 0, 0)),
                  pl.BlockSpec((SB_HEADS, LANES), lambda b, j, pt: (0, 0)),
                  pl.BlockSpec((PAGE_SIZE, PAGE_SIZE), lambda b, j, pt: (0, 0))]
                 + [page_spec(p) for p in range(pp)] * 2,
        out_specs=pl.BlockSpec((1, 1, SB_WIDTH), lambda b, j, pt: (b, 0, 0)),
        scratch_shapes=[pltpu.VMEM((SB_HEADS, SB_WIDTH), F32), pltpu.VMEM((SB_HEADS, LANES), F32)],
    )
    return pl.pallas_call(
        functools.partial(_attn_decode_kernel, pp=pp),
        grid_spec=grid_spec,
        out_shape=jax.ShapeDtypeStruct((nseq, 1, SB_WIDTH), BF16),
        compiler_params=_cparams(("parallel", "arbitrary")),
        name="sb_attn_decode",
    )(page_table, q16, bias_col, tri, *([cache_k] * pp), *([cache_v] * pp))


DN_PACK = DN_HEADS * DN_CHUNK
CONV_PAD = SUBLANES


def _dot_tn(a, b):
    return lax.dot_general(a, b, (((0,), (0,)), ((), ())), preferred_element_type=F32)


def _delta_kernel(a_ref, ab_ref, cw_ref, alog_ref, dtb_ref, dnn_ref, e2_ref, lmat_ref, bd1_ref,
                  conv0_ref, s0_ref, o_ref, convn_ref, sout_ref, xp_scr, s_scr,
                  *, tb, nv_last, mask_rows):
    c = DN_CHUNK
    w3 = 3 * DN_WIDTH
    j = pl.program_id(1)

    @pl.when(j == 0)
    def _():
        xp_scr[0:CONV_PAD, :] = conv0_ref[...]
        s_scr[...] = s0_ref[...]

    xp_scr[CONV_PAD:CONV_PAD + tb, :] = a_ref[:, 0:w3]
    off = CONV_PAD - (DN_CONV - 1)
    conv = cw_ref[0:1, :] * xp_scr[off:off + tb, :]
    for i in range(1, DN_CONV):
        conv = conv + cw_ref[i:i + 1, :] * xp_scr[off + i:off + i + tb, :]
    convn_ref[...] = xp_scr[nv_last:nv_last + CONV_PAD, :]
    xp_scr[0:CONV_PAD, :] = xp_scr[tb:tb + CONV_PAD, :]
    r = _silu(conv)

    ii = lax.broadcasted_iota(jnp.int32, (c, DN_PACK), 0)
    jj = lax.broadcasted_iota(jnp.int32, (c, DN_PACK), 1) % c
    strict = jj < ii
    incl = jj <= ii
    bdmask = (lax.broadcasted_iota(jnp.int32, (DN_PACK, DN_PACK), 0) // c
              == lax.broadcasted_iota(jnp.int32, (DN_PACK, DN_PACK), 1) // c)
    lane8 = lax.broadcasted_iota(jnp.int32, (c, LANES), 1)
    bd1 = bd1_ref[...]

    def bd(x16):
        return jnp.where(bdmask, jnp.concatenate([x16] * DN_HEADS, axis=0), jnp.zeros((), BF16))

    def pp(a2t, b2t):
        return _dot(a2t[0], b2t[0]) + _dot(a2t[0], b2t[1]) + _dot(a2t[1], b2t[0])

    def split_bd(x):
        return tuple(bd(t) for t in _split(x, 2))

    def level_mask(b):
        return (((ii // (2 * b)) == (jj // (2 * b))) & ((ii % (2 * b)) >= b) & ((jj % (2 * b)) < b))

    chunks = range(tb // c)
    eye = (ii == jj).astype(F32)
    levels = []
    b = 1
    while b < c:
        levels.append(level_mask(b))
        b *= 2
    qs, ks, vs, bexps, egams, kdecs, qkms, nmats = [], [], [], [], [], [], [], []
    for ci in chunks:
        rows = slice(ci * c, (ci + 1) * c)
        q = r[rows, 0:DN_WIDTH]
        k = r[rows, DN_WIDTH:2 * DN_WIDTH]
        v = r[rows, 2 * DN_WIDTH:w3]
        q = q * lax.rsqrt(_dot_r01(q * q, bd1, 1) + EPS) * (DN_HEAD_DIM ** -0.5)
        k = k * lax.rsqrt(_dot_r01(k * k, bd1, 1) + EPS)
        abv = ab_ref[rows, :]
        g_all = -jnp.exp(alog_ref[...]) * _softplus(abv + dtb_ref[...])
        gb = jnp.where(lane8 < DN_HEADS, g_all, _sigmoid(abv))
        if mask_rows:
            valid = lax.broadcasted_iota(jnp.int32, (c, 1), 0) < (nv_last - ci * c)
            q = jnp.where(valid, q, 0.0)
            k = jnp.where(valid, k, 0.0)
            v = jnp.where(valid, v, 0.0)
            gb = jnp.where(valid, gb, 0.0)
        gbx = _dot_r01(gb, e2_ref[...], 3)
        gexp = gbx[:, 0:DN_PACK]
        bexp = gbx[:, DN_PACK:]
        cums = _dot_l01(lmat_ref[...], gexp, 3)
        egam = jnp.exp(cums[0:c, :])
        kdec = k * jnp.exp(cums[c:, :])
        dmat = _dot_l01(lmat_ref[0:c, :], jnp.where(strict, gexp, 0.0), 3)
        decay = jnp.exp(jnp.where(incl, dmat, 0.0))
        k16 = k.astype(BF16)
        bdk = bd(k16)
        kk = _dot_nt(k16, bdk)
        qk = _dot_nt(q.astype(BF16), bdk)
        nmats.append(jnp.where(strict, bexp * kk * decay, 0.0))
        qkms.append(jnp.where(incl, qk * decay, 0.0))
        qs.append(q)
        ks.append(k)
        vs.append(v)
        bexps.append(bexp)
        egams.append(egam)
        kdecs.append(kdec)

    xs = [eye - jnp.where(levels[0], nm, 0.0) for nm in nmats]
    nsplit = [_split(nm, 2) for nm in nmats]
    zero16 = jnp.zeros((), BF16)
    for lm in levels[1:]:
        xsplit = [_split(x, 2) for x in xs]
        ts = [pp(xsplit[ci], tuple(bd(jnp.where(lm, n, zero16)) for n in nsplit[ci])) for ci in chunks]
        xs = [xs[ci] - pp(_split(ts[ci], 2), tuple(bd(t) for t in xsplit[ci])) for ci in chunks]
    xsplit = [_split(x, 2) for x in xs]
    us = [pp(xsplit[ci], split_bd(bexps[ci] * vs[ci])) for ci in chunks]
    ws = [pp(xsplit[ci], split_bd(bexps[ci] * egams[ci] * ks[ci])) for ci in chunks]
    prepped = [(us[ci], ws[ci], qkms[ci], qs[ci] * egams[ci], kdecs[ci], egams[ci][c - 1:c, :],
                a_ref[ci * c:(ci + 1) * c, w3:w3 + DN_WIDTH]) for ci in chunks]

    s = s_scr[...]
    for ci, (u, w, qkm, qdec, kdec, glrow, gate) in enumerate(prepped):
        s16 = s.astype(BF16)
        vnew = u - _dot(w.astype(BF16), s16)
        v16 = vnew.astype(BF16)
        o = _dot(qdec.astype(BF16), s16) + _dot(qkm.astype(BF16), bd(v16))
        s = glrow * s + jnp.where(bdmask, _dot_tn(kdec.astype(BF16), v16), 0.0)
        ms = _dot_r01(o * o, bd1, 2) * (1.0 / DN_HEAD_DIM)
        o = o * lax.rsqrt(ms + EPS) * dnn_ref[...] * _silu(gate)
        o_ref[ci * c:(ci + 1) * c, :] = o.astype(BF16)
    s_scr[...] = s
    sout_ref[...] = s


def _delta(a, ab, cw, alog, dtb, dnn, e2, lmat, bd1, conv0, s0, bsz, t_pad, t_valid, tb):
    nblk = t_pad // tb
    nv_last = t_valid - (nblk - 1) * tb
    full = lambda x: pl.BlockSpec(x.shape, lambda b, j: (0,) * x.ndim)
    per_b = lambda x: pl.BlockSpec((None,) + x.shape[1:], lambda b, j: (b,) + (0,) * (x.ndim - 1))
    return pl.pallas_call(
        functools.partial(_delta_kernel, tb=tb, nv_last=nv_last, mask_rows=(t_valid != t_pad)),
        grid=(bsz, nblk),
        in_specs=[pl.BlockSpec((tb, A_COLS), lambda b, j: (b * nblk + j, 0)),
                  pl.BlockSpec((tb, LANES), lambda b, j: (b * nblk + j, 0)),
                  full(cw), full(alog), full(dtb), full(dnn), full(e2), full(lmat), full(bd1),
                  per_b(conv0), per_b(s0)],
        out_specs=(pl.BlockSpec((tb, DN_WIDTH), lambda b, j: (b * nblk + j, 0)), per_b(conv0), per_b(s0)),
        out_shape=(jax.ShapeDtypeStruct((bsz * t_pad, DN_WIDTH), BF16),
                   jax.ShapeDtypeStruct(conv0.shape, F32),
                   jax.ShapeDtypeStruct(s0.shape, F32)),
        scratch_shapes=[pltpu.VMEM((CONV_PAD + tb, 3 * DN_WIDTH), F32), pltpu.VMEM((DN_PACK, DN_PACK), F32)],
        compiler_params=_cparams(("parallel", "arbitrary")),
        name="gated_delta",
    )(a, ab, cw, alog, dtb, dnn, e2, lmat, bd1, conv0, s0)


def _delta_step_kernel(a_ref, ab_ref, cw_ref, alog_ref, dtb_ref, dnn_ref, e2_ref, bd1_ref, conv_ref, s_ref,
                       o_ref, convn_ref, sout_ref):
    nseq = a_ref.shape[0]
    w3 = 3 * DN_WIDTH
    x = a_ref[:, 0:w3]
    conv = cw_ref[DN_CONV - 1:DN_CONV, :] * x
    for i in range(DN_CONV - 1):
        conv = conv + cw_ref[i:i + 1, :] * conv_ref[i]
        if i > 0:
            convn_ref[i - 1] = conv_ref[i]
    convn_ref[DN_CONV - 2] = x
    r = _silu(conv)
    bd1 = bd1_ref[...]
    q = r[:, 0:DN_WIDTH]
    k = r[:, DN_WIDTH:2 * DN_WIDTH]
    v = r[:, 2 * DN_WIDTH:w3]
    q = q * lax.rsqrt(_dot_r01(q * q, bd1, 2) + EPS) * (DN_HEAD_DIM ** -0.5)
    k = k * lax.rsqrt(_dot_r01(k * k, bd1, 2) + EPS)
    abv = ab_ref[...]
    lane8 = lax.broadcasted_iota(jnp.int32, abv.shape, 1)
    g_all = -jnp.exp(alog_ref[...]) * _softplus(abv + dtb_ref[...])
    gbx = _dot_r01(jnp.where(lane8 < DN_HEADS, g_all, _sigmoid(abv)), e2_ref[...], 3)
    eg = jnp.exp(gbx[:, 0:DN_PACK])
    bexp = gbx[:, DN_PACK:]
    qk = _dot_r01(q * k, bd1, 2)
    wk = bexp * eg * k
    qd = q * eg
    row = lax.broadcasted_iota(jnp.int32, (nseq, DN_PACK), 0)
    bdmask = (lax.broadcasted_iota(jnp.int32, (DN_PACK, DN_PACK), 0) // DN_HEAD_DIM
              == lax.broadcasted_iota(jnp.int32, (DN_PACK, DN_PACK), 1) // DN_HEAD_DIM)

    def only(s, x):
        return jnp.where(row == s, x, 0.0).astype(BF16)

    s16 = [s_ref[s].astype(BF16) for s in range(nseq)]
    ws = sum(_dot(only(s, wk), s16[s]) for s in range(nseq))
    qs = sum(_dot(only(s, qd), s16[s]) for s in range(nseq))
    vnew = bexp * v - ws
    o = qs + qk * vnew
    ms = _dot_r01(o * o, bd1, 2) * (1.0 / DN_HEAD_DIM)
    o = o * lax.rsqrt(ms + EPS) * dnn_ref[...] * _silu(a_ref[:, w3:w3 + DN_WIDTH])
    o_ref[...] = o.astype(BF16)
    v16 = vnew.astype(BF16)
    for s in range(nseq):
        outer = _dot_tn(only(s, k), v16)
        sout_ref[s] = eg[s:s + 1, :] * s_ref[s] + jnp.where(bdmask, outer, 0.0)


def _delta_step(a, ab, cw, alog, dtb, dnn, e2, bd1, conv_t, s0):
    nseq = a.shape[0]
    blk = SUBLANES
    full = lambda x: pl.BlockSpec(x.shape, lambda i: (0,) * x.ndim)
    conv_spec = pl.BlockSpec((DN_CONV - 1, blk, 3 * DN_WIDTH), lambda i: (0, i, 0))
    s_spec = pl.BlockSpec((blk, DN_PACK, DN_PACK), lambda i: (i, 0, 0))
    return pl.pallas_call(
        _delta_step_kernel,
        grid=(nseq // blk,),
        in_specs=[pl.BlockSpec((blk, A_COLS), lambda i: (i, 0)), pl.BlockSpec((blk, LANES), lambda i: (i, 0)),
                  full(cw), full(alog), full(dtb), full(dnn), full(e2), full(bd1), conv_spec, s_spec],
        out_specs=(pl.BlockSpec((blk, DN_WIDTH), lambda i: (i, 0)), conv_spec, s_spec),
        out_shape=(jax.ShapeDtypeStruct((nseq, DN_WIDTH), BF16),
                   jax.ShapeDtypeStruct(conv_t.shape, F32),
                   jax.ShapeDtypeStruct(s0.shape, F32)),
        compiler_params=_cparams(("parallel",)),
        name="gated_delta_step",
    )(a, ab, cw, alog, dtb, dnn, e2, bd1, conv_t, s0)


def _delta_constants():
    c = DN_CHUNK
    lane = jnp.arange(2 * DN_PACK)
    row = jnp.arange(LANES)[:, None]
    e2 = (((lane[None, :] < DN_PACK) & (row == lane[None, :] // c))
          | ((lane[None, :] >= DN_PACK) & (row == DN_HEADS + (lane[None, :] - DN_PACK) // c))).astype(BF16)
    i = jnp.arange(c)[:, None]
    m = jnp.arange(c)[None, :]
    lmat = jnp.concatenate([(m <= i), (m > i)], axis=0).astype(BF16)
    hh = jnp.arange(DN_PACK) // c
    bd1 = (hh[:, None] == hh[None, :]).astype(BF16)
    return e2, lmat, bd1


def _tri_incl(n):
    return (jnp.arange(n)[:, None] >= jnp.arange(n)[None, :]).astype(BF16)


def _seg_matrices():
    c = jnp.arange(2 * SB_WIDTH)[:, None] // SB_HEAD_DIM
    j = jnp.arange(LANES)[None, :]
    seg = (c == j).astype(BF16)
    return seg, seg.T


def _prep_w_in(w):
    n_ab = 2 * DN_HEADS
    c1 = A_COLS
    c2 = c1 + n_ab
    c3 = c2 + B_COLS
    pad = jnp.zeros((D_MODEL, LANES - n_ab), w.dtype)
    return jnp.concatenate([w[:, :c1], w[:, c2:c3], w[:, c3:], w[:, c1:c2], pad], axis=1).astype(BF16)


def _row(v, width):
    return jnp.pad(v.astype(F32), (0, width - v.shape[0]))[None]


def _pack_state(s):
    eye = jnp.eye(DN_HEADS, dtype=s.dtype)
    return jnp.einsum('bhkv,hg->bhkgv', s, eye).reshape(s.shape[0], DN_PACK, DN_PACK)


def _unpack_state(sp):
    b = sp.shape[0]
    sp = sp.reshape(b, DN_HEADS, DN_HEAD_DIM, DN_HEADS, DN_HEAD_DIM)
    return jnp.stack([sp[:, h, :, h, :] for h in range(DN_HEADS)], axis=1)


TM_PROJ = 256
TM_FFN = 512
TH_FFN = 6 * MXU_TILE
TB_DELTA = 8 * DN_CHUNK
TQ_ATTN = 512
S5_ROWS = 1024


def _mixers(x, lw, consts, bsz, t, conv0, s0, x0_ssm, attn_fn, layer, kv_bufs):
    (w16, qkn, cw, alog, dtb, dnn, wb, lam, cm, dskip, gw16, gb, wo16, ffn_g, wg16, wu16, wd16, attn_g) = lw
    seg, segt, e2, lmat, bd1 = consts
    n = bsz * t
    tm = min(TM_PROJ, n)
    a, ab, q16, k16, v16, kf, vf, u, u_time_major = _proj_in(x, attn_g, w16, qkn, seg, segt, tm, bsz, t,
                                                             layer, kv_bufs)

    hist = CONV_PAD - (DN_CONV - 1)
    if t == 1 and bsz % SUBLANES == 0:
        o_a, conv_t, s_new = _delta_step(a, ab, cw, alog, dtb, dnn, e2, bd1,
                                         conv0[:, hist:].transpose(1, 0, 2), s0)
        convn = jnp.pad(conv_t.transpose(1, 0, 2), ((0, 0), (hist, 0), (0, 0)))
    else:
        tb = min(TB_DELTA, -(-t // DN_CHUNK) * DN_CHUNK)
        t_pad = -(-t // tb) * tb
        if t_pad != t:
            padt = lambda y: jnp.pad(y.reshape(bsz, t, -1), ((0, 0), (0, t_pad - t), (0, 0))).reshape(bsz * t_pad, -1)
            a_in, ab_in = padt(a), padt(ab)
        else:
            a_in, ab_in = a, ab
        o_a, convn, s_new = _delta(a_in, ab_in, cw, alog, dtb, dnn, e2, lmat, bd1, conv0, s0, bsz, t_pad, t, tb)
        if t_pad != t:
            o_a = o_a.reshape(bsz, t_pad, DN_WIDTH)[:, :t].reshape(n, DN_WIDTH)

    o_b = attn_fn(q16, k16, v16)

    nb = bsz if (bsz * 2 == SUBLANES and t % 2 == 0) else -(-bsz // SUBLANES) * SUBLANES
    steps = min(S5_ROWS // nb, t)
    tm_ffn = min(TM_FFN, n)
    u_tb = u.reshape(t, bsz, SSM_WIDTH) if u_time_major else u.reshape(bsz, t, SSM_WIDTH).transpose(1, 0, 2)
    if nb != bsz:
        u_tb = jnp.pad(u_tb, ((0, 0), (0, nb - bsz), (0, 0)))
    o_c, xt = _s5(u_tb.reshape(t * nb, SSM_WIDTH), wb, lam, cm, dskip, gw16, gb,
                  jnp.pad(x0_ssm, ((0, nb - bsz), (0, 0))), steps, nb)
    direct = u_time_major and nb == bsz and t % tm_ffn == 0
    if not direct:
        o_c = o_c.reshape(t, nb, SSM_WIDTH)[:, :bsz].transpose(1, 0, 2).reshape(n, SSM_WIDTH)

    x = _out_ffn(x, o_a, o_b, o_c, wo16, ffn_g, wg16, wu16, wd16, tm_ffn, TH_FFN, t if direct else None)
    leaves = (convn[:, CONV_PAD - (DN_CONV - 1):], _unpack_state(s_new),
              xt[:bsz, :SSM_FLAT].reshape(bsz, SSM_GROUPS, SSM_STATE),
              xt[:bsz, SSM_FLAT:].reshape(bsz, SSM_GROUPS, SSM_STATE))
    return x, (kf, vf), leaves


def kernel(x_prompt, x_sample, cache_k, cache_v, page_table, state_conv, state_delta, state_ssm_re, state_ssm_im,
           attn_norm, w_in, conv_w, dn_a_log, dn_dt_bias, dn_out_norm, sb_q_norm, sb_k_norm, sb_bias,
           ssm_lambda_re, ssm_lambda_im, ssm_log_step, ssm_b_re, ssm_b_im, ssm_c_re, ssm_c_im, ssm_d,
           ssm_glu_w, ssm_glu_b, w_out, ffn_norm, w_gate, w_up, w_down):
    bp, tp, _ = x_prompt.shape
    bs, ts, _ = x_sample.shape
    assert ts == 1, "the sample group decodes one token per sequence"
    n_pool = cache_k.shape[1]
    ck = cache_k.transpose(0, 1, 3, 4, 2).reshape(DEPTH, n_pool, SB_WIDTH, PAGE_SIZE)
    cv = cache_v.transpose(0, 1, 3, 4, 2).reshape(DEPTH, n_pool, SB_WIDTH, PAGE_SIZE)
    consts = _seg_matrices() + _delta_constants()
    tri_p = _tri_incl(TQ_ATTN)
    tri_d = _tri_incl(PAGE_SIZE)

    xp = x_prompt.reshape(bp * tp, D_MODEL)
    xs = x_sample.reshape(bs * ts, D_MODEL)
    conv0_p = jnp.zeros((bp, CONV_PAD, 3 * DN_WIDTH), F32)
    s0_p = jnp.zeros((bp, DN_PACK, DN_PACK), F32)
    x0_p = jnp.zeros((bp, 2 * SSM_FLAT), F32)
    acc = [[] for _ in range(8)]
    kv_p = kv_s = None
    for l in range(DEPTH):
        wb, cm, lam = _prep_s5(ssm_lambda_re[l], ssm_lambda_im[l], ssm_log_step[l],
                               ssm_b_re[l], ssm_b_im[l], ssm_c_re[l], ssm_c_im[l])
        qkn = jnp.concatenate([jnp.tile(sb_q_norm[l], SB_HEADS), jnp.tile(sb_k_norm[l], SB_HEADS)])[None]
        lw = (_prep_w_in(w_in[l]), qkn, conv_w[l], _row(dn_a_log[l], LANES), _row(dn_dt_bias[l], LANES),
              jnp.tile(dn_out_norm[l], DN_HEADS)[None], wb, lam, cm, ssm_d[l][None],
              ssm_glu_w[l].astype(BF16), ssm_glu_b[l][None], w_out[l].astype(BF16), ffn_norm[l][None],
              w_gate[l].astype(BF16), w_up[l].astype(BF16), w_down[l].astype(BF16), attn_norm[l][None])
        bias = sb_bias[l].astype(F32)

        attn_p = lambda q, k, v: _attn_prompt(q, k, v, bias, tri_p, bp, tp, TQ_ATTN)
        xp, kv_p, st_p = _mixers(xp, lw, consts, bp, tp, conv0_p, s0_p, x0_p, attn_p, l, kv_p)

        bias_col = jnp.broadcast_to(bias[:, None], (SB_HEADS, LANES))
        attn_s = lambda q, k, v: _attn_decode(q.reshape(bs, 1, SB_WIDTH), bias_col, tri_d, page_table,
                                              ck, cv, l).reshape(bs, SB_WIDTH)
        conv0_s = jnp.pad(state_conv[l], ((0, 0), (CONV_PAD - (DN_CONV - 1), 0), (0, 0)))
        x0_s = jnp.concatenate([state_ssm_re[l].reshape(bs, SSM_FLAT), state_ssm_im[l].reshape(bs, SSM_FLAT)], axis=1)
        xs, kv_s, st_s = _mixers(xs, lw, consts, bs, ts, conv0_s, _pack_state(state_delta[l]), x0_s, attn_s, l, kv_s)
        for a, val in zip(acc, st_p + st_s):
            a.append(val)
    st = [jnp.stack(a, axis=0) for a in acc]
    def heads(buf, b, t):
        if buf.ndim == 4:
            return buf.reshape(DEPTH, b, SB_HEADS, SB_HEAD_DIM, t).transpose(0, 1, 4, 2, 3)
        return buf.reshape(DEPTH, b, t, SB_HEADS, SB_HEAD_DIM)

    return (xp.reshape(bp, tp, D_MODEL), xs.reshape(bs, ts, D_MODEL),
            heads(kv_p[0], bp, tp), heads(kv_p[1], bp, tp), *st[:4],
            heads(kv_s[0], bs, ts), heads(kv_s[1], bs, ts), *st[4:])
```

```python
import functools
import math

import jax
import jax.numpy as jnp
from jax import lax
from jax.experimental import pallas as pl
from jax.experimental.pallas import tpu as pltpu

F32 = jnp.float32
BF16 = jnp.bfloat16

D_MODEL = 1024
DEPTH = 4
PAGE_SIZE = 128
DN_HEADS = 4
DN_HEAD_DIM = 64
DN_WIDTH = DN_HEADS * DN_HEAD_DIM
DN_CONV = 4
DN_CHUNK = 64
SB_HEADS = 8
SB_HEAD_DIM = 64
SB_WIDTH = SB_HEADS * SB_HEAD_DIM
SSM_WIDTH = D_MODEL - DN_WIDTH - SB_WIDTH
SSM_GROUP = 16
SSM_GROUPS = SSM_WIDTH // SSM_GROUP
SSM_STATE = 64
SSM_FLAT = SSM_GROUPS * SSM_STATE
FFN_HIDDEN = 2816
EPS = 1e-6

LANES = 128
SUBLANES = 8
VMEM_LIMIT = 56 * 1024 * 1024

A_COLS = 4 * DN_WIDTH
B_COLS = 3 * SB_WIDTH
IN_PAD = A_COLS + B_COLS + SSM_WIDTH + LANES


def _cparams(sem):
    return pltpu.CompilerParams(dimension_semantics=sem, vmem_limit_bytes=VMEM_LIMIT)


def _split(x, n):
    terms = []
    r = x
    for i in range(n):
        t = r.astype(BF16)
        terms.append(t)
        if i + 1 < n:
            r = r - t.astype(F32)
    return terms


def _dot(a, b):
    return jnp.dot(a, b, preferred_element_type=F32)


def _dot_l01(a01, x, n):
    return sum(_dot(a01, t) for t in _split(x, n))


def _dot_r01(x, b01, n):
    return sum(_dot(t, b01) for t in _split(x, n))


def _sigmoid(x):
    return 1.0 / (1.0 + jnp.exp(-x))


def _silu(x):
    return x * _sigmoid(x)


LOG2E = 1.4426950408889634


def _softplus(x):
    return jnp.maximum(x, 0.0) + jnp.log(1.0 + jnp.exp2(jnp.abs(x) * (-LOG2E)))


def _proj_in_kernel(x_ref, g_ref, w_ref, qkn_ref, seg_ref, segt_ref, *refs, feature_major):
    a_ref, ab_ref, q16_ref, k16_ref, v16_ref, kf_ref, vf_ref, u_ref = refs[-8:]
    x = x_ref[...]
    ms = jnp.mean(x * x, axis=-1, keepdims=True)
    h = (x * lax.rsqrt(ms + EPS) * g_ref[...]).astype(BF16)
    a_ref[...] = _dot(h, w_ref[:, 0:A_COLS])
    c0 = A_COLS
    qk = _dot(h, w_ref[:, c0:c0 + 2 * SB_WIDTH])
    v = _dot(h, w_ref[:, c0 + 2 * SB_WIDTH:c0 + B_COLS])
    c0 += B_COLS
    u_ref[...] = _dot(h, w_ref[:, c0:c0 + SSM_WIDTH])
    c0 += SSM_WIDTH
    ab_ref[...] = _dot(h, w_ref[:, c0:c0 + LANES])
    ssq = _dot_r01(qk * qk, seg_ref[...], 1)
    inv = lax.rsqrt(ssq * (1.0 / SB_HEAD_DIM) + EPS)
    qkn = qk * _dot_r01(inv, segt_ref[...], 2) * qkn_ref[...]
    qn = qkn[:, 0:SB_WIDTH]
    kn = qkn[:, SB_WIDTH:]
    q16_ref[...] = (qn * (SB_HEAD_DIM ** -0.5)).astype(BF16)
    k16_ref[...] = kn.astype(BF16)
    v16_ref[...] = v.astype(BF16)
    if feature_major:
        kf_ref[...] = kn.T
        vf_ref[...] = v.T
    else:
        kf_ref[...] = kn
        vf_ref[...] = v


def _proj_in(x, gain, w16, qkn, seg, segt, tm, bsz, t, layer, kv_bufs):
    n = x.shape[0]
    row = lambda w: pl.BlockSpec((tm, w), lambda i: (i, 0))
    full = lambda a: pl.BlockSpec(a.shape, lambda i: (0,) * a.ndim)
    feature_major = t % tm == 0 and tm % LANES == 0
    if feature_major:
        nt = t // tm
        slab = pl.BlockSpec((None, None, SB_WIDTH, tm), lambda i: (layer, i // nt, 0, i % nt))
        kv_shape = jax.ShapeDtypeStruct((DEPTH, bsz, SB_WIDTH, t), F32)
        u_spec = pl.BlockSpec((tm, SSM_WIDTH), lambda i: (i % nt, i // nt))
        u_shape = jax.ShapeDtypeStruct((t, bsz * SSM_WIDTH), F32)
    else:
        slab = pl.BlockSpec((None, tm, SB_WIDTH), lambda i: (layer, i, 0))
        kv_shape = jax.ShapeDtypeStruct((DEPTH, n, SB_WIDTH), F32)
        u_spec = row(SSM_WIDTH)
        u_shape = jax.ShapeDtypeStruct((n, SSM_WIDTH), F32)
    out_shapes = (
        jax.ShapeDtypeStruct((n, A_COLS), F32),
        jax.ShapeDtypeStruct((n, LANES), F32),
        jax.ShapeDtypeStruct((n, SB_WIDTH), BF16),
        jax.ShapeDtypeStruct((n, SB_WIDTH), BF16),
        jax.ShapeDtypeStruct((n, SB_WIDTH), BF16),
        kv_shape,
        kv_shape,
        u_shape,
    )
    if kv_bufs is None:
        kv_bufs = (jnp.zeros(kv_shape.shape, F32), jnp.zeros(kv_shape.shape, F32))
    in_specs = ([row(D_MODEL), full(gain), full(w16), full(qkn), full(seg), full(segt)]
                + [pl.BlockSpec(memory_space=pl.ANY)] * 2)
    args = [x, gain, w16, qkn, seg, segt, *kv_bufs]
    aliases = {len(args) - 2: 5, len(args) - 1: 6}
    outs = pl.pallas_call(
        functools.partial(_proj_in_kernel, feature_major=feature_major),
        grid=(n // tm,),
        in_specs=in_specs,
        out_specs=(row(A_COLS), row(LANES), row(SB_WIDTH), row(SB_WIDTH), row(SB_WIDTH),
                   slab, slab, u_spec),
        out_shape=out_shapes,
        input_output_aliases=aliases,
        compiler_params=_cparams(("parallel",)),
        name="proj_in",
    )(*args)
    return outs[:7] + (outs[7].reshape(n, SSM_WIDTH), feature_major)


MXU_TILE = 256


def _hidden_splits(th_max):
    assert FFN_HIDDEN % MXU_TILE == 0
    tiles = FFN_HIDDEN // MXU_TILE
    per = max(th_max // MXU_TILE, 1)
    cuts = list(range(0, tiles, per)) + [tiles]
    return [(lo * MXU_TILE, hi * MXU_TILE) for lo, hi in zip(cuts[:-1], cuts[1:])]


def _out_ffn_kernel(x_ref, oa_ref, ob_ref, oc_ref, wo_ref, g_ref, wg_ref, wu_ref, wd_ref, y_ref, *, th):
    mix = (_dot(oa_ref[...], wo_ref[0:DN_WIDTH, :])
           + _dot(ob_ref[...], wo_ref[DN_WIDTH:DN_WIDTH + SB_WIDTH, :])
           + _dot(oc_ref[...], wo_ref[DN_WIDTH + SB_WIDTH:, :]))
    x1 = x_ref[...] + mix
    ms = jnp.mean(x1 * x1, axis=-1, keepdims=True)
    h = (x1 * lax.rsqrt(ms + EPS) * g_ref[...]).astype(BF16)
    y = x1
    for lo, hi in _hidden_splits(th):
        gate = _dot(h, wg_ref[:, lo:hi])
        up = _dot(h, wu_ref[:, lo:hi])
        y = y + _dot((_silu(gate) * up).astype(BF16), wd_ref[lo:hi, :])
    y_ref[...] = y


def _out_ffn(x, oa, ob, oc, wo16, gain, wg16, wu16, wd16, tm, th, oc_time_major_t=None):
    n = x.shape[0]
    row = lambda w: pl.BlockSpec((tm, w), lambda i: (i, 0))
    fixed = lambda a: pl.BlockSpec(a.shape, lambda i: (0,) * a.ndim, pipeline_mode=pl.Buffered(1))
    oc_spec = row(SSM_WIDTH)
    if oc_time_major_t is not None:
        t = oc_time_major_t
        nt = t // tm
        oc = oc.reshape(t, (n // t) * SSM_WIDTH)
        oc_spec = pl.BlockSpec((tm, SSM_WIDTH), lambda i: (i % nt, i // nt))
    return pl.pallas_call(
        functools.partial(_out_ffn_kernel, th=th),
        grid=(n // tm,),
        in_specs=[row(D_MODEL), row(DN_WIDTH), row(SB_WIDTH), oc_spec,
                  fixed(wo16), fixed(gain), fixed(wg16), fixed(wu16), fixed(wd16)],
        out_specs=row(D_MODEL),
        out_shape=jax.ShapeDtypeStruct((n, D_MODEL), F32),
        compiler_params=_cparams(("parallel",)),
        name="out_ffn",
    )(x, oa, ob, oc, wo16, gain, wg16, wu16, wd16)


def _gelu_tanh(x):
    c = math.sqrt(2.0 / math.pi)
    return 0.5 * x * (1.0 + jnp.tanh(c * (x + 0.044715 * (x * x * x))))


def _s5_kernel(u_ref, wb_ref, lam_ref, cm_ref, d_ref, gw_ref, gb_ref, x0_ref,
               o_ref, xt_ref, bu_scr, x_scr, *, steps, nb):
    two_step = nb * 2 == SUBLANES
    tile = SUBLANES if two_step else nb

    @pl.when(pl.program_id(0) == 0)
    def _():
        x_scr[...] = jnp.concatenate([x0_ref[...]] * 2, axis=0) if two_step else x0_ref[...]

    u = u_ref[...]
    bu_scr[...] = _dot(u.astype(BF16), wb_ref[...])
    lre = jnp.broadcast_to(lam_ref[:, 0:SSM_FLAT], (tile, SSM_FLAT))
    lim = jnp.broadcast_to(lam_ref[:, SSM_FLAT:], (tile, SSM_FLAT))
    lower = lax.broadcasted_iota(jnp.int32, (tile, 2 * SSM_FLAT), 0) < nb

    def advance(x, bu):
        xr = x[:, 0:SSM_FLAT]
        xi = x[:, SSM_FLAT:]
        nr = lre * xr - lim * xi + bu[:, 0:SSM_FLAT]
        ni = lre * xi + lim * xr + bu[:, SSM_FLAT:]
        return jnp.concatenate([nr, ni], axis=1)

    def step(t, x):
        r0 = pl.multiple_of(t * tile, tile)
        bu = bu_scr[pl.ds(r0, tile), :]
        x1 = advance(x, bu)
        if two_step:
            x2 = advance(pltpu.roll(x1, nb, 0), bu)
            bu_scr[pl.ds(r0, tile), :] = jnp.where(lower, x1, x2)
            return jnp.where(lower, pltpu.roll(x2, nb, 0), x2)
        bu_scr[pl.ds(r0, tile), :] = x1
        return x1

    x = lax.fori_loop(0, steps * nb // tile, step, x_scr[...])
    x_scr[...] = x
    xt_ref[...] = x[0:nb, :]
    y = _dot(bu_scr[...].astype(BF16), cm_ref[...]) + d_ref[...] * u
    z = _gelu_tanh(y)
    o_ref[...] = (z * _sigmoid(_dot(z.astype(BF16), gw_ref[...]) + gb_ref[...])).astype(BF16)


def _s5(u_rows, wb, lam, cm, d, gw16, gb, x0, steps, nb):
    assert nb % SUBLANES == 0 or (nb * 2 == SUBLANES and steps % 2 == 0)
    n = u_rows.shape[0]
    rows = steps * nb
    full = lambda a: pl.BlockSpec(a.shape, lambda i: (0,) * a.ndim)
    return pl.pallas_call(
        functools.partial(_s5_kernel, steps=steps, nb=nb),
        grid=(n // rows,),
        in_specs=[pl.BlockSpec((rows, SSM_WIDTH), lambda i: (i, 0)),
                  full(wb), full(lam), full(cm), full(d), full(gw16), full(gb), full(x0)],
        out_specs=(pl.BlockSpec((rows, SSM_WIDTH), lambda i: (i, 0)), full(x0)),
        out_shape=(jax.ShapeDtypeStruct((n, SSM_WIDTH), BF16),
                   jax.ShapeDtypeStruct(x0.shape, F32)),
        scratch_shapes=[pltpu.VMEM((rows, 2 * SSM_FLAT), F32),
                        pltpu.VMEM((max(nb, SUBLANES), 2 * SSM_FLAT), F32)],
        compiler_params=_cparams(("arbitrary",)),
        name="s5_scan",
    )(u_rows, wb, lam, cm, d, gw16, gb, x0)


def _prep_s5(lam_re, lam_im, log_step, b_re, b_im, c_re, c_im):
    step = jnp.exp(log_step)[:, None]
    mag = jnp.exp(lam_re * step)
    ang = lam_im * step
    lb_re, lb_im = mag * jnp.cos(ang), mag * jnp.sin(ang)
    den = lam_re * lam_re + lam_im * lam_im
    f_re = ((lb_re - 1.0) * lam_re + lb_im * lam_im) / den
    f_im = (lb_im * lam_re - (lb_re - 1.0) * lam_im) / den
    bb_re = f_re[..., None] * b_re - f_im[..., None] * b_im
    bb_im = f_re[..., None] * b_im + f_im[..., None] * b_re
    eye = jnp.eye(SSM_GROUPS, dtype=F32)
    pack_b = lambda bb: jnp.einsum('gpc,gh->gchp', bb, eye).reshape(SSM_WIDTH, SSM_FLAT)
    pack_c = lambda cc: jnp.einsum('gcp,gh->gphc', cc, eye).reshape(SSM_FLAT, SSM_WIDTH)
    wb = jnp.concatenate([pack_b(bb_re), pack_b(bb_im)], axis=1).astype(BF16)
    cm = jnp.concatenate([pack_c(c_re), -pack_c(c_im)], axis=0).astype(BF16)
    lam = jnp.concatenate([lb_re.reshape(1, SSM_FLAT), lb_im.reshape(1, SSM_FLAT)], axis=1)
    return wb, cm, lam


HEAD_GROUP = 4
GROUP_W = HEAD_GROUP * SB_HEAD_DIM


def _dot_nt(a, b):
    return lax.dot_general(a, b, (((1,), (1,)), ((), ())), preferred_element_type=F32)


def _attn_kernel(bias_ref, q_ref, k_ref, v_ref, tri_ref, o_ref, acc_scr, z_scr, w_scr, vs_scr, *, tq):
    g = pl.program_id(1)
    i = pl.program_id(2)
    q = q_ref[...]
    lane_head = lax.broadcasted_iota(jnp.int32, (tq, GROUP_W), 1) // SB_HEAD_DIM

    @pl.when(i == 0)
    def _():
        def fill(j, _):
            vb = v_ref[pl.ds(pl.multiple_of(j * tq, tq), tq), :]
            for hh in range(HEAD_GROUP):
                vs_scr[j, hh * tq:(hh + 1) * tq, :] = jnp.where(lane_head == hh, vb, jnp.zeros_like(vb))
            return 0
        lax.fori_loop(0, vs_scr.shape[0], fill, 0)
    causal = (lax.broadcasted_iota(jnp.int32, (tq, tq), 1)
              < lax.broadcasted_iota(jnp.int32, (tq, tq), 0))
    tri = tri_ref[...]
    heads = range(HEAD_GROUP)
    qms = [jnp.where(lane_head == hh, q, jnp.zeros_like(q)) for hh in heads]
    biases = [bias_ref[g * HEAD_GROUP + hh] for hh in heads]

    r0 = pl.multiple_of(i * tq, tq)

    def suffix_sum(sp):
        return _dot(sp.astype(BF16), tri)

    def scores(c0):
        kb = k_ref[pl.ds(c0, tq), :]
        return [_dot_nt(qms[hh], kb) + biases[hh] for hh in heads]

    zs = scores(r0)
    cs = [suffix_sum(jnp.where(causal, _softplus(zs[hh]), 0.0)) for hh in heads]
    for hh in heads:
        w_scr[:, hh * tq:(hh + 1) * tq] = jnp.where(causal, jnp.exp(zs[hh] - cs[hh]), 0.0).astype(BF16)
    acc_scr[...] = jnp.zeros((tq, GROUP_W), F32)
    zn = scores(pl.multiple_of(jnp.maximum(i - 1, 0) * tq, tq))
    for hh in heads:
        z_scr[hh] = zn[hh]
    carries = tuple(cb[:, 0:1] for cb in cs)

    def body(jj, carries):
        zc = [z_scr[hh] for hh in heads]
        zn = scores(pl.multiple_of(jnp.maximum(i - jj - 1, 0) * tq, tq))
        acc_scr[...] += _dot(w_scr[...], vs_scr[i - jj + 1])
        cs = [suffix_sum(_softplus(zc[hh])) + carries[hh] for hh in heads]
        for hh in heads:
            w_scr[:, hh * tq:(hh + 1) * tq] = jnp.exp(zc[hh] - cs[hh]).astype(BF16)
            z_scr[hh] = zn[hh]
        return tuple(cb[:, 0:1] for cb in cs)

    lax.fori_loop(1, i + 1, body, carries)
    o_ref[...] = (acc_scr[...] + _dot(w_scr[...], vs_scr[0])).astype(BF16)


def _attn_prompt(q16, k16, v16, bias, tri, bsz, t, tq):
    nq = t // tq
    ngrp = SB_WIDTH // GROUP_W
    grid_spec = pltpu.PrefetchScalarGridSpec(
        num_scalar_prefetch=1,
        grid=(bsz, ngrp, nq),
        in_specs=[pl.BlockSpec((tq, GROUP_W), lambda b, g, i, s: (b * nq + i, g)),
                  pl.BlockSpec((t, GROUP_W), lambda b, g, i, s: (b, g)),
                  pl.BlockSpec((t, GROUP_W), lambda b, g, i, s: (b, g)),
                  pl.BlockSpec((tq, tq), lambda b, g, i, s: (0, 0))],
        out_specs=pl.BlockSpec((tq, GROUP_W), lambda b, g, i, s: (b * nq + i, g)),
        scratch_shapes=[pltpu.VMEM((tq, GROUP_W), F32),
                        pltpu.VMEM((HEAD_GROUP, tq, tq), F32),
                        pltpu.VMEM((tq, HEAD_GROUP * tq), BF16),
                        pltpu.VMEM((nq, HEAD_GROUP * tq, GROUP_W), BF16)],
    )
    return pl.pallas_call(
        functools.partial(_attn_kernel, tq=tq),
        grid_spec=grid_spec,
        out_shape=jax.ShapeDtypeStruct((bsz * t, SB_WIDTH), BF16),
        compiler_params=_cparams(("arbitrary", "arbitrary", "arbitrary")),
        name="sb_attn_prompt",
    )(bias, q16, k16, v16, tri)


PAGES_PER_STEP = 32


def _attn_decode_kernel(pt_ref, q_ref, bias_ref, tri_ref, *refs, pp):
    del pt_ref
    k_refs = refs[:pp]
    v_refs = refs[pp:2 * pp]
    o_ref, acc_scr, carry_scr = refs[2 * pp:]
    j = pl.program_id(1)

    @pl.when(j == 0)
    def _():
        acc_scr[...] = jnp.zeros_like(acc_scr)
        carry_scr[...] = jnp.zeros_like(carry_scr)

    on_diag = (lax.broadcasted_iota(jnp.int32, (SB_HEADS, SB_WIDTH), 0)
               == lax.broadcasted_iota(jnp.int32, (SB_HEADS, SB_WIDTH), 1) // SB_HEAD_DIM)
    q = jnp.broadcast_to(q_ref[0].astype(F32), (SB_HEADS, SB_WIDTH))
    qbd = jnp.where(on_diag, q, 0.0).astype(BF16)
    bias = bias_ref[...]
    tri = tri_ref[...]
    kcat = jnp.concatenate([k_refs[p][...].astype(BF16) for p in range(pp)], axis=1)
    zw = _dot(qbd, kcat)
    z = jnp.concatenate([zw[:, p * PAGE_SIZE:(p + 1) * PAGE_SIZE] + bias for p in range(pp)], axis=0)
    cum = _dot_r01(_softplus(z), tri, 2)
    carry = carry_scr[:, 0:1]
    offs = []
    for p in range(pp):
        offs.append(carry)
        carry = carry + cum[p * SB_HEADS:(p + 1) * SB_HEADS, 0:1]
    w = jnp.exp(z - cum - jnp.concatenate(offs, axis=0)).astype(BF16)
    wcat = jnp.concatenate([w[p * SB_HEADS:(p + 1) * SB_HEADS, :] for p in range(pp)], axis=1)
    vcat = jnp.concatenate([v_refs[p][...].astype(BF16) for p in range(pp)], axis=1)
    acc = acc_scr[...] + _dot_nt(wcat, vcat)
    acc_scr[...] = acc
    carry_scr[...] = jnp.broadcast_to(carry, carry_scr.shape)

    @pl.when(j == pl.num_programs(1) - 1)
    def _():
        o_ref[0] = jnp.sum(jnp.where(on_diag, acc, 0.0), axis=0, keepdims=True).astype(BF16)


def _attn_decode(q16, bias_col, tri, page_table, cache_k, cache_v, layer):
    nseq, npg = page_table.shape
    pp = PAGES_PER_STEP

    def page_spec(p):
        return pl.BlockSpec((None, None, SB_WIDTH, PAGE_SIZE),
                            lambda b, j, pt: (layer, pt[b, npg - 1 - (j * pp + p)], 0, 0))

    grid_spec = pltpu.PrefetchScalarGridSpec(
        num_scalar_prefetch=1,
        grid=(nseq, npg // pp),
        in_specs=[pl.BlockSpec((1, 1, SB_WIDTH), lambda b, j, pt: (b, 0, 0)),
                  pl.BlockSpec((SB_HEADS, LANES), lambda b, j, pt: (0, 0)),
                  pl.BlockSpec((PAGE_SIZE, PAGE_SIZE), lambda b, j, pt: (0, 0))]
                 + [page_spec(p) for p in range(pp)] * 2,
        out_specs=pl.BlockSpec((1, 1, SB_WIDTH), lambda b, j, pt: (b, 0, 0)),
        scratch_shapes=[pltpu.VMEM((SB_HEADS, SB_WIDTH), F32), pltpu.VMEM((SB_HEADS, LANES), F32)],
    )
    return pl.pallas_call(
        functools.partial(_attn_decode_kernel, pp=pp),
        grid_spec=grid_spec,
        out_shape=jax.ShapeDtypeStruct((nseq, 1, SB_WIDTH), BF16),
        compiler_params=_cparams(("parallel", "arbitrary")),
        name="sb_attn_decode",
    )(page_table, q16, bias_col, tri, *([cache_k] * pp), *([cache_v] * pp))


DN_PACK = DN_HEADS * DN_CHUNK
CONV_PAD = SUBLANES


def _dot_tn(a, b):
    return lax.dot_general(a, b, (((0,), (0,)), ((), ())), preferred_element_type=F32)


def _delta_kernel(a_ref, ab_ref, cw_ref, alog_ref, dtb_ref, dnn_ref, e2_ref, lmat_ref, bd1_ref,
                  conv0_ref, s0_ref, o_ref, convn_ref, sout_ref, xp_scr, s_scr,
                  *, tb, nv_last, mask_rows):
    c = DN_CHUNK
    w3 = 3 * DN_WIDTH
    j = pl.program_id(1)

    @pl.when(j == 0)
    def _():
        xp_scr[0:CONV_PAD, :] = conv0_ref[...]
        s_scr[...] = s0_ref[...]

    xp_scr[CONV_PAD:CONV_PAD + tb, :] = a_ref[:, 0:w3]
    off = CONV_PAD - (DN_CONV - 1)
    conv = cw_ref[0:1, :] * xp_scr[off:off + tb, :]
    for i in range(1, DN_CONV):
        conv = conv + cw_ref[i:i + 1, :] * xp_scr[off + i:off + i + tb, :]
    convn_ref[...] = xp_scr[nv_last:nv_last + CONV_PAD, :]
    xp_scr[0:CONV_PAD, :] = xp_scr[tb:tb + CONV_PAD, :]
    r = _silu(conv)

    ii = lax.broadcasted_iota(jnp.int32, (c, DN_PACK), 0)
    jj = lax.broadcasted_iota(jnp.int32, (c, DN_PACK), 1) % c
    strict = jj < ii
    incl = jj <= ii
    bdmask = (lax.broadcasted_iota(jnp.int32, (DN_PACK, DN_PACK), 0) // c
              == lax.broadcasted_iota(jnp.int32, (DN_PACK, DN_PACK), 1) // c)
    lane8 = lax.broadcasted_iota(jnp.int32, (c, LANES), 1)
    bd1 = bd1_ref[...]

    def bd(x16):
        return jnp.where(bdmask, jnp.concatenate([x16] * DN_HEADS, axis=0), jnp.zeros((), BF16))

    def pp(a2t, b2t):
        return _dot(a2t[0], b2t[0]) + _dot(a2t[0], b2t[1]) + _dot(a2t[1], b2t[0])

    def split_bd(x):
        return tuple(bd(t) for t in _split(x, 2))

    def level_mask(b):
        return (((ii // (2 * b)) == (jj // (2 * b))) & ((ii % (2 * b)) >= b) & ((jj % (2 * b)) < b))

    chunks = range(tb // c)
    eye = (ii == jj).astype(F32)
    levels = []
    b = 1
    while b < c:
        levels.append(level_mask(b))
        b *= 2
    qs, ks, vs, bexps, egams, kdecs, qkms, nmats = [], [], [], [], [], [], [], []
    for ci in chunks:
        rows = slice(ci * c, (ci + 1) * c)
        q = r[rows, 0:DN_WIDTH]
        k = r[rows, DN_WIDTH:2 * DN_WIDTH]
        v = r[rows, 2 * DN_WIDTH:w3]
        q = q * lax.rsqrt(_dot_r01(q * q, bd1, 1) + EPS) * (DN_HEAD_DIM ** -0.5)
        k = k * lax.rsqrt(_dot_r01(k * k, bd1, 1) + EPS)
        abv = ab_ref[rows, :]
        g_all = -jnp.exp(alog_ref[...]) * _softplus(abv + dtb_ref[...])
        gb = jnp.where(lane8 < DN_HEADS, g_all, _sigmoid(abv))
        if mask_rows:
            valid = lax.broadcasted_iota(jnp.int32, (c, 1), 0) < (nv_last - ci * c)
            q = jnp.where(valid, q, 0.0)
            k = jnp.where(valid, k, 0.0)
            v = jnp.where(valid, v, 0.0)
            gb = jnp.where(valid, gb, 0.0)
        gbx = _dot_r01(gb, e2_ref[...], 3)
        gexp = gbx[:, 0:DN_PACK]
        bexp = gbx[:, DN_PACK:]
        cums = _dot_l01(lmat_ref[...], gexp, 3)
        egam = jnp.exp(cums[0:c, :])
        kdec = k * jnp.exp(cums[c:, :])
        dmat = _dot_l01(lmat_ref[0:c, :], jnp.where(strict, gexp, 0.0), 3)
        decay = jnp.exp(jnp.where(incl, dmat, 0.0))
        k16 = k.astype(BF16)
        bdk = bd(k16)
        kk = _dot_nt(k16, bdk)
        qk = _dot_nt(q.astype(BF16), bdk)
        nmats.append(jnp.where(strict, bexp * kk * decay, 0.0))
        qkms.append(jnp.where(incl, qk * decay, 0.0))
        qs.append(q)
        ks.append(k)
        vs.append(v)
        bexps.append(bexp)
        egams.append(egam)
        kdecs.append(kdec)

    xs = [eye - jnp.where(levels[0], nm, 0.0) for nm in nmats]
    nsplit = [_split(nm, 2) for nm in nmats]
    zero16 = jnp.zeros((), BF16)
    for lm in levels[1:]:
        xsplit = [_split(x, 2) for x in xs]
        ts = [pp(xsplit[ci], tuple(bd(jnp.where(lm, n, zero16)) for n in nsplit[ci])) for ci in chunks]
        xs = [xs[ci] - pp(_split(ts[ci], 2), tuple(bd(t) for t in xsplit[ci])) for ci in chunks]
    xsplit = [_split(x, 2) for x in xs]
    us = [pp(xsplit[ci], split_bd(bexps[ci] * vs[ci])) for ci in chunks]
    ws = [pp(xsplit[ci], split_bd(bexps[ci] * egams[ci] * ks[ci])) for ci in chunks]
    prepped = [(us[ci], ws[ci], qkms[ci], qs[ci] * egams[ci], kdecs[ci], egams[ci][c - 1:c, :],
                a_ref[ci * c:(ci + 1) * c, w3:w3 + DN_WIDTH]) for ci in chunks]

    s = s_scr[...]
    for ci, (u, w, qkm, qdec, kdec, glrow, gate) in enumerate(prepped):
        s16 = s.astype(BF16)
        vnew = u - _dot(w.astype(BF16), s16)
        v16 = vnew.astype(BF16)
        o = _dot(qdec.astype(BF16), s16) + _dot(qkm.astype(BF16), bd(v16))
        s = glrow * s + jnp.where(bdmask, _dot_tn(kdec.astype(BF16), v16), 0.0)
        ms = _dot_r01(o * o, bd1, 2) * (1.0 / DN_HEAD_DIM)
        o = o * lax.rsqrt(ms + EPS) * dnn_ref[...] * _silu(gate)
        o_ref[ci * c:(ci + 1) * c, :] = o.astype(BF16)
    s_scr[...] = s
    sout_ref[...] = s


def _delta(a, ab, cw, alog, dtb, dnn, e2, lmat, bd1, conv0, s0, bsz, t_pad, t_valid, tb):
    nblk = t_pad // tb
    nv_last = t_valid - (nblk - 1) * tb
    full = lambda x: pl.BlockSpec(x.shape, lambda b, j: (0,) * x.ndim)
    per_b = lambda x: pl.BlockSpec((None,) + x.shape[1:], lambda b, j: (b,) + (0,) * (x.ndim - 1))
    return pl.pallas_call(
        functools.partial(_delta_kernel, tb=tb, nv_last=nv_last, mask_rows=(t_valid != t_pad)),
        grid=(bsz, nblk),
        in_specs=[pl.BlockSpec((tb, A_COLS), lambda b, j: (b * nblk + j, 0)),
                  pl.BlockSpec((tb, LANES), lambda b, j: (b * nblk + j, 0)),
                  full(cw), full(alog), full(dtb), full(dnn), full(e2), full(lmat), full(bd1),
                  per_b(conv0), per_b(s0)],
        out_specs=(pl.BlockSpec((tb, DN_WIDTH), lambda b, j: (b * nblk + j, 0)), per_b(conv0), per_b(s0)),
        out_shape=(jax.ShapeDtypeStruct((bsz * t_pad, DN_WIDTH), BF16),
                   jax.ShapeDtypeStruct(conv0.shape, F32),
                   jax.ShapeDtypeStruct(s0.shape, F32)),
        scratch_shapes=[pltpu.VMEM((CONV_PAD + tb, 3 * DN_WIDTH), F32), pltpu.VMEM((DN_PACK, DN_PACK), F32)],
        compiler_params=_cparams(("parallel", "arbitrary")),
        name="gated_delta",
    )(a, ab, cw, alog, dtb, dnn, e2, lmat, bd1, conv0, s0)


def _delta_step_kernel(a_ref, ab_ref, cw_ref, alog_ref, dtb_ref, dnn_ref, e2_ref, bd1_ref, conv_ref, s_ref,
                       o_ref, convn_ref, sout_ref):
    nseq = a_ref.shape[0]
    w3 = 3 * DN_WIDTH
    x = a_ref[:, 0:w3]
    conv = cw_ref[DN_CONV - 1:DN_CONV, :] * x
    for i in range(DN_CONV - 1):
        conv = conv + cw_ref[i:i + 1, :] * conv_ref[i]
        if i > 0:
            convn_ref[i - 1] = conv_ref[i]
    convn_ref[DN_CONV - 2] = x
    r = _silu(conv)
    bd1 = bd1_ref[...]
    q = r[:, 0:DN_WIDTH]
    k = r[:, DN_WIDTH:2 * DN_WIDTH]
    v = r[:, 2 * DN_WIDTH:w3]
    q = q * lax.rsqrt(_dot_r01(q * q, bd1, 2) + EPS) * (DN_HEAD_DIM ** -0.5)
    k = k * lax.rsqrt(_dot_r01(k * k, bd1, 2) + EPS)
    abv = ab_ref[...]
    lane8 = lax.broadcasted_iota(jnp.int32, abv.shape, 1)
    g_all = -jnp.exp(alog_ref[...]) * _softplus(abv + dtb_ref[...])
    gbx = _dot_r01(jnp.where(lane8 < DN_HEADS, g_all, _sigmoid(abv)), e2_ref[...], 3)
    eg = jnp.exp(gbx[:, 0:DN_PACK])
    bexp = gbx[:, DN_PACK:]
    qk = _dot_r01(q * k, bd1, 2)
    wk = bexp * eg * k
    qd = q * eg
    row = lax.broadcasted_iota(jnp.int32, (nseq, DN_PACK), 0)
    bdmask = (lax.broadcasted_iota(jnp.int32, (DN_PACK, DN_PACK), 0) // DN_HEAD_DIM
              == lax.broadcasted_iota(jnp.int32, (DN_PACK, DN_PACK), 1) // DN_HEAD_DIM)

    def only(s, x):
        return jnp.where(row == s, x, 0.0).astype(BF16)

    s16 = [s_ref[s].astype(BF16) for s in range(nseq)]
    ws = sum(_dot(only(s, wk), s16[s]) for s in range(nseq))
    qs = sum(_dot(only(s, qd), s16[s]) for s in range(nseq))
    vnew = bexp * v - ws
    o = qs + qk * vnew
    ms = _dot_r01(o * o, bd1, 2) * (1.0 / DN_HEAD_DIM)
    o = o * lax.rsqrt(ms + EPS) * dnn_ref[...] * _silu(a_ref[:, w3:w3 + DN_WIDTH])
    o_ref[...] = o.astype(BF16)
    v16 = vnew.astype(BF16)
    for s in range(nseq):
        outer = _dot_tn(only(s, k), v16)
        sout_ref[s] = eg[s:s + 1, :] * s_ref[s] + jnp.where(bdmask, outer, 0.0)


def _delta_step(a, ab, cw, alog, dtb, dnn, e2, bd1, conv_t, s0):
    nseq = a.shape[0]
    blk = SUBLANES
    full = lambda x: pl.BlockSpec(x.shape, lambda i: (0,) * x.ndim)
    conv_spec = pl.BlockSpec((DN_CONV - 1, blk, 3 * DN_WIDTH), lambda i: (0, i, 0))
    s_spec = pl.BlockSpec((blk, DN_PACK, DN_PACK), lambda i: (i, 0, 0))
    return pl.pallas_call(
        _delta_step_kernel,
        grid=(nseq // blk,),
        in_specs=[pl.BlockSpec((blk, A_COLS), lambda i: (i, 0)), pl.BlockSpec((blk, LANES), lambda i: (i, 0)),
                  full(cw), full(alog), full(dtb), full(dnn), full(e2), full(bd1), conv_spec, s_spec],
        out_specs=(pl.BlockSpec((blk, DN_WIDTH), lambda i: (i, 0)), conv_spec, s_spec),
        out_shape=(jax.ShapeDtypeStruct((nseq, DN_WIDTH), BF16),
                   jax.ShapeDtypeStruct(conv_t.shape, F32),
                   jax.ShapeDtypeStruct(s0.shape, F32)),
        compiler_params=_cparams(("parallel",)),
        name="gated_delta_step",
    )(a, ab, cw, alog, dtb, dnn, e2, bd1, conv_t, s0)


def _delta_constants():
    c = DN_CHUNK
    lane = jnp.arange(2 * DN_PACK)
    row = jnp.arange(LANES)[:, None]
    e2 = (((lane[None, :] < DN_PACK) & (row == lane[None, :] // c))
          | ((lane[None, :] >= DN_PACK) & (row == DN_HEADS + (lane[None, :] - DN_PACK) // c))).astype(BF16)
    i = jnp.arange(c)[:, None]
    m = jnp.arange(c)[None, :]
    lmat = jnp.concatenate([(m <= i), (m > i)], axis=0).astype(BF16)
    hh = jnp.arange(DN_PACK) // c
    bd1 = (hh[:, None] == hh[None, :]).astype(BF16)
    return e2, lmat, bd1


def _tri_incl(n):
    return (jnp.arange(n)[:, None] >= jnp.arange(n)[None, :]).astype(BF16)


def _seg_matrices():
    c = jnp.arange(2 * SB_WIDTH)[:, None] // SB_HEAD_DIM
    j = jnp.arange(LANES)[None, :]
    seg = (c == j).astype(BF16)
    return seg, seg.T


def _prep_w_in(w):
    n_ab = 2 * DN_HEADS
    c1 = A_COLS
    c2 = c1 + n_ab
    c3 = c2 + B_COLS
    pad = jnp.zeros((D_MODEL, LANES - n_ab), w.dtype)
    return jnp.concatenate([w[:, :c1], w[:, c2:c3], w[:, c3:], w[:, c1:c2], pad], axis=1).astype(BF16)


def _row(v, width):
    return jnp.pad(v.astype(F32), (0, width - v.shape[0]))[None]


def _pack_state(s):
    eye = jnp.eye(DN_HEADS, dtype=s.dtype)
    return jnp.einsum('bhkv,hg->bhkgv', s, eye).reshape(s.shape[0], DN_PACK, DN_PACK)


def _unpack_state(sp):
    b = sp.shape[0]
    sp = sp.reshape(b, DN_HEADS, DN_HEAD_DIM, DN_HEADS, DN_HEAD_DIM)
    return jnp.stack([sp[:, h, :, h, :] for h in range(DN_HEADS)], axis=1)


TM_PROJ = 256
TM_FFN = 512
TH_FFN = 6 * MXU_TILE
TB_DELTA = 8 * DN_CHUNK
TQ_ATTN = 512
S5_ROWS = 1024


def _mixers(x, lw, consts, bsz, t, conv0, s0, x0_ssm, attn_fn, layer, kv_bufs):
    (w16, qkn, cw, alog, dtb, dnn, wb, lam, cm, dskip, gw16, gb, wo16, ffn_g, wg16, wu16, wd16, attn_g) = lw
    seg, segt, e2, lmat, bd1 = consts
    n = bsz * t
    tm = min(TM_PROJ, n)
    a, ab, q16, k16, v16, kf, vf, u, u_time_major = _proj_in(x, attn_g, w16, qkn, seg, segt, tm, bsz, t,
                                                             layer, kv_bufs)

    hist = CONV_PAD - (DN_CONV - 1)
    if t == 1 and bsz % SUBLANES == 0:
        o_a, conv_t, s_new = _delta_step(a, ab, cw, alog, dtb, dnn, e2, bd1,
                                         conv0[:, hist:].transpose(1, 0, 2), s0)
        convn = jnp.pad(conv_t.transpose(1, 0, 2), ((0, 0), (hist, 0), (0, 0)))
    else:
        tb = min(TB_DELTA, -(-t // DN_CHUNK) * DN_CHUNK)
        t_pad = -(-t // tb) * tb
        if t_pad != t:
            padt = lambda y: jnp.pad(y.reshape(bsz, t, -1), ((0, 0), (0, t_pad - t), (0, 0))).reshape(bsz * t_pad, -1)
            a_in, ab_in = padt(a), padt(ab)
        else:
            a_in, ab_in = a, ab
        o_a, convn, s_new = _delta(a_in, ab_in, cw, alog, dtb, dnn, e2, lmat, bd1, conv0, s0, bsz, t_pad, t, tb)
        if t_pad != t:
            o_a = o_a.reshape(bsz, t_pad, DN_WIDTH)[:, :t].reshape(n, DN_WIDTH)

    o_b = attn_fn(q16, k16, v16)

    nb = bsz if (bsz * 2 == SUBLANES and t % 2 == 0) else -(-bsz // SUBLANES) * SUBLANES
    steps = min(S5_ROWS // nb, t)
    tm_ffn = min(TM_FFN, n)
    u_tb = u.reshape(t, bsz, SSM_WIDTH) if u_time_major else u.reshape(bsz, t, SSM_WIDTH).transpose(1, 0, 2)
    if nb != bsz:
        u_tb = jnp.pad(u_tb, ((0, 0), (0, nb - bsz), (0, 0)))
    o_c, xt = _s5(u_tb.reshape(t * nb, SSM_WIDTH), wb, lam, cm, dskip, gw16, gb,
                  jnp.pad(x0_ssm, ((0, nb - bsz), (0, 0))), steps, nb)
    direct = u_time_major and nb == bsz and t % tm_ffn == 0
    if not direct:
        o_c = o_c.reshape(t, nb, SSM_WIDTH)[:, :bsz].transpose(1, 0, 2).reshape(n, SSM_WIDTH)

    x = _out_ffn(x, o_a, o_b, o_c, wo16, ffn_g, wg16, wu16, wd16, tm_ffn, TH_FFN, t if direct else None)
    leaves = (convn[:, CONV_PAD - (DN_CONV - 1):], _unpack_state(s_new),
              xt[:bsz, :SSM_FLAT].reshape(bsz, SSM_GROUPS, SSM_STATE),
              xt[:bsz, SSM_FLAT:].reshape(bsz, SSM_GROUPS, SSM_STATE))
    return x, (kf, vf), leaves


def kernel(x_prompt, x_sample, cache_k, cache_v, page_table, state_conv, state_delta, state_ssm_re, state_ssm_im,
           attn_norm, w_in, conv_w, dn_a_log, dn_dt_bias, dn_out_norm, sb_q_norm, sb_k_norm, sb_bias,
           ssm_lambda_re, ssm_lambda_im, ssm_log_step, ssm_b_re, ssm_b_im, ssm_c_re, ssm_c_im, ssm_d,
           ssm_glu_w, ssm_glu_b, w_out, ffn_norm, w_gate, w_up, w_down):
    bp, tp, _ = x_prompt.shape
    bs, ts, _ = x_sample.shape
    assert ts == 1, "the sample group decodes one token per sequence"
    n_pool = cache_k.shape[1]
    ck = cache_k.transpose(0, 1, 3, 4, 2).reshape(DEPTH, n_pool, SB_WIDTH, PAGE_SIZE)
    cv = cache_v.transpose(0, 1, 3, 4, 2).reshape(DEPTH, n_pool, SB_WIDTH, PAGE_SIZE)
    consts = _seg_matrices() + _delta_constants()
    tri_p = _tri_incl(TQ_ATTN)
    tri_d = _tri_incl(PAGE_SIZE)

    xp = x_prompt.reshape(bp * tp, D_MODEL)
    xs = x_sample.reshape(bs * ts, D_MODEL)
    conv0_p = jnp.zeros((bp, CONV_PAD, 3 * DN_WIDTH), F32)
    s0_p = jnp.zeros((bp, DN_PACK, DN_PACK), F32)
    x0_p = jnp.zeros((bp, 2 * SSM_FLAT), F32)
    acc = [[] for _ in range(8)]
    kv_p = kv_s = None
    for l in range(DEPTH):
        wb, cm, lam = _prep_s5(ssm_lambda_re[l], ssm_lambda_im[l], ssm_log_step[l],
                               ssm_b_re[l], ssm_b_im[l], ssm_c_re[l], ssm_c_im[l])
        qkn = jnp.concatenate([jnp.tile(sb_q_norm[l], SB_HEADS), jnp.tile(sb_k_norm[l], SB_HEADS)])[None]
        lw = (_prep_w_in(w_in[l]), qkn, conv_w[l], _row(dn_a_log[l], LANES), _row(dn_dt_bias[l], LANES),
              jnp.tile(dn_out_norm[l], DN_HEADS)[None], wb, lam, cm, ssm_d[l][None],
              ssm_glu_w[l].astype(BF16), ssm_glu_b[l][None], w_out[l].astype(BF16), ffn_norm[l][None],
              w_gate[l].astype(BF16), w_up[l].astype(BF16), w_down[l].astype(BF16), attn_norm[l][None])
        bias = sb_bias[l].astype(F32)

        attn_p = lambda q, k, v: _attn_prompt(q, k, v, bias, tri_p, bp, tp, TQ_ATTN)
        xp, kv_p, st_p = _mixers(xp, lw, consts, bp, tp, conv0_p, s0_p, x0_p, attn_p, l, kv_p)

        bias_col = jnp.broadcast_to(bias[:, None], (SB_HEADS, LANES))
        attn_s = lambda q, k, v: _attn_decode(q.reshape(bs, 1, SB_WIDTH), bias_col, tri_d, page_table,
                                              ck, cv, l).reshape(bs, SB_WIDTH)
        conv0_s = jnp.pad(state_conv[l], ((0, 0), (CONV_PAD - (DN_CONV - 1), 0), (0, 0)))
        x0_s = jnp.concatenate([state_ssm_re[l].reshape(bs, SSM_FLAT), state_ssm_im[l].reshape(bs, SSM_FLAT)], axis=1)
        xs, kv_s, st_s = _mixers(xs, lw, consts, bs, ts, conv0_s, _pack_state(state_delta[l]), x0_s, attn_s, l, kv_s)
        for a, val in zip(acc, st_p + st_s):
            a.append(val)
    st = [jnp.stack(a, axis=0) for a in acc]
    def heads(buf, b, t):
        if buf.ndim == 4:
            return buf.reshape(DEPTH, b, SB_HEADS, SB_HEAD_DIM, t).transpose(0, 1, 4, 2, 3)
        return buf.reshape(DEPTH, b, t, SB_HEADS, SB_HEAD_DIM)

    return (xp.reshape(bp, tp, D_MODEL), xs.reshape(bs, ts, D_MODEL),
            heads(kv_p[0], bp, tp), heads(kv_p[1], bp, tp), *st[:4],
            heads(kv_s[0], bs, ts), heads(kv_s[1], bs, ts), *st[4:])
```
